```python
import math
import jax, jax.numpy as jnp
from jax import lax
import numpy as np

D_MODEL = 1024
BATCH = 2
SEQ = 8192
DEPTH = 4
DEC_BATCH = 16
DEC_SEQ = 64
PAST_LEN = 1024

CHUNK = 64
N_PREV_CHUNKS = 8
A_HEADS = 16
A_HEAD_DIM = D_MODEL // A_HEADS
REL_CLIP = 128
B_HEADS = 16
B_HEAD_DIM = D_MODEL // B_HEADS
IDX_HEADS = 8
IDX_DIM = 64
TOPK_MAX = 256
Q_BLOCK = 128
D_FF = 4 * D_MODEL
N_A_LAYERS = (DEPTH + 1) // 2
N_B_LAYERS = DEPTH // 2
DN_ALPHA = (2.0 * DEPTH) ** 0.25
DN_BETA = (8.0 * DEPTH) ** -0.25
LN_EPS = 1e-5
B_IN_WIDTH = 3 * D_MODEL + IDX_HEADS * IDX_DIM + IDX_DIM + IDX_HEADS

kernel_name = 'streaming_chunkband_dsa_hybrid'


def _alibi_slopes(n):
    return jnp.asarray(np.array([2.0 ** (-8.0 * (h + 1) / n) for h in range(n)], np.float32))


def _layer_norm(x, g, b):
    xf = x.astype(jnp.float32)
    mu = jnp.mean(xf, axis=-1, keepdims=True)
    var = jnp.mean(jnp.square(xf - mu), axis=-1, keepdims=True)
    y = (xf - mu) * lax.rsqrt(var + LN_EPS) * g.astype(jnp.float32) + b.astype(jnp.float32)
    return y.astype(x.dtype)


def _ffn(x, w1, w2):
    h = jax.nn.relu(x @ w1)
    return (h * h) @ w2


def _softmax_attend(q, k, v, bias, valid):
    s = jnp.einsum('...qhd,...khd->...hqk', q, k).astype(jnp.float32) * (q.shape[-1] ** -0.5) + bias
    s = jnp.where(valid, s, -jnp.inf)
    p = jax.nn.softmax(s, axis=-1).astype(v.dtype)
    return jnp.einsum('...hqk,...khd->...qhd', p, v)


def _rel_bias(tab, rel):
    return tab[:, jnp.clip(rel, -REL_CLIP, REL_CLIP) + REL_CLIP].astype(jnp.float32)


def _a_project(x, w_qkv):
    bsz, t, _ = x.shape
    q, k, v = jnp.split(x @ w_qkv, 3, axis=-1)
    shp = (bsz, t, A_HEADS, A_HEAD_DIM)
    return q.reshape(shp), k.reshape(shp), v.reshape(shp)


def mixer_a_prompt(x, w_qkv, w_o, rel_tab):
    bsz, t, _ = x.shape
    nc = t // CHUNK
    band = N_PREV_CHUNKS + 1
    q, k, v = _a_project(x, w_qkv)
    cshape = (bsz, nc, CHUNK, A_HEADS, A_HEAD_DIM)
    pad = ((0, 0), (N_PREV_CHUNKS, 0), (0, 0), (0, 0), (0, 0))
    kc = jnp.pad(k.reshape(cshape), pad)
    vc = jnp.pad(v.reshape(cshape), pad)
    k_band = jnp.concatenate([kc[:, j:j + nc] for j in range(band)], axis=2)
    v_band = jnp.concatenate([vc[:, j:j + nc] for j in range(band)], axis=2)
    q_off = jnp.arange(CHUNK)
    k_off = jnp.arange(band * CHUNK)
    rel = q_off[:, None] + N_PREV_CHUNKS * CHUNK - k_off[None, :]
    bias = _rel_bias(rel_tab, rel)
    k_chunk = jnp.arange(nc)[:, None] - N_PREV_CHUNKS + k_off[None, :] // CHUNK
    valid = (k_chunk >= 0)[:, None, None, :]
    o = _softmax_attend(q.reshape(cshape), k_band, v_band, bias, valid)
    y = o.reshape(bsz, t, D_MODEL) @ w_o
    n_keep = min(N_PREV_CHUNKS * CHUNK, t)
    return y, k[:, t - n_keep:], v[:, t - n_keep:]


def mixer_a_sample(x, cache_k, cache_v, w_qkv, w_o, rel_tab):
    bsz, s_len, _ = x.shape
    a_len = cache_k.shape[1]
    q, k, v = _a_project(x, w_qkv)
    k_all = jnp.concatenate([cache_k, k], axis=1)
    v_all = jnp.concatenate([cache_v, v], axis=1)
    q_pos = PAST_LEN + jnp.arange(s_len)
    k_pos = jnp.concatenate([PAST_LEN - a_len + jnp.arange(a_len), q_pos])
    bias = _rel_bias(rel_tab, q_pos[:, None] - k_pos[None, :])
    qch = q_pos // CHUNK
    kch = k_pos // CHUNK
    valid = (kch[None, :] <= qch[:, None]) & (kch[None, :] >= qch[:, None] - N_PREV_CHUNKS)
    o = _softmax_attend(q, k_all, v_all, bias, valid)
    return o.reshape(bsz, s_len, D_MODEL) @ w_o, k, v


def _b_project(x, w_in):
    bsz, t, _ = x.shape
    z = x @ w_in
    c1 = 3 * D_MODEL
    c2 = c1 + IDX_HEADS * IDX_DIM
    c3 = c2 + IDX_DIM
    q, k, v = jnp.split(z[..., :c1], 3, axis=-1)
    shp = (bsz, t, B_HEADS, B_HEAD_DIM)
    qi = z[..., c1:c2].reshape(bsz, t, IDX_HEADS, IDX_DIM)
    ki = z[..., c2:c3]
    wi = z[..., c3:]
    return q.reshape(shp), k.reshape(shp), v.reshape(shp), qi, ki, wi


def _dsa_attend(q, qi, wi, q_pos, k, v, ki, k_pos, n_sel, slopes):
    logits = jnp.einsum('bqhd,bld->bqhl', qi, ki).astype(jnp.float32)
    score = jnp.einsum('bqh,bqhl->bql', wi.astype(jnp.float32), jax.nn.relu(logits)) * (IDX_DIM ** -0.5 * IDX_HEADS ** -0.5)
    adm = (k_pos[None, :] // CHUNK) <= (q_pos[:, None] // CHUNK)
    score = jnp.where(adm[None], score, -jnp.inf)
    _, sel = lax.top_k(score, n_sel)
    gather = jax.vmap(lambda a, i: a[i])
    k_sel = gather(k, sel)
    v_sel = gather(v, sel)
    s_pos = k_pos[sel]
    valid = (s_pos // CHUNK) <= (q_pos[None, :, None] // CHUNK)
    dist = jnp.abs(q_pos[None, :, None] - s_pos).astype(jnp.float32)
    s = jnp.einsum('bqhd,bqnhd->bhqn', q, k_sel).astype(jnp.float32) * (B_HEAD_DIM ** -0.5)
    s = s - slopes[None, :, None, None] * dist[:, None]
    s = jnp.where(valid[:, None], s, -jnp.inf)
    p = jax.nn.softmax(s, axis=-1).astype(v.dtype)
    return jnp.einsum('bhqn,bqnhd->bqhd', p, v_sel)


def mixer_b_prompt(x, w_in, w_o, slopes):
    bsz, t, _ = x.shape
    q, k, v, qi, ki, wi = _b_project(x, w_in)
    pos = jnp.arange(t)
    n_sel = min(TOPK_MAX, t // 4)
    nb = t // Q_BLOCK

    def to_blocks(a):
        return jnp.moveaxis(a.reshape((bsz, nb, Q_BLOCK) + a.shape[2:]), 1, 0)

    def blk(args):
        qb, qib, wib, pb = args
        return _dsa_attend(qb, qib, wib, pb, k, v, ki, pos, n_sel, slopes)

    o = lax.map(blk, (to_blocks(q), to_blocks(qi), to_blocks(wi), pos.reshape(nb, Q_BLOCK)))
    o = jnp.moveaxis(o, 0, 1).reshape(bsz, t, D_MODEL)
    return o @ w_o, k, v, ki


def mixer_b_sample(x, cache_k, cache_v, cache_ki, w_in, w_o, slopes):
    bsz, s_len, _ = x.shape
    past = cache_k.shape[1]
    q, k, v, qi, ki, wi = _b_project(x, w_in)
    k_all = jnp.concatenate([cache_k, k], axis=1)
    v_all = jnp.concatenate([cache_v, v], axis=1)
    ki_all = jnp.concatenate([cache_ki, ki], axis=1)
    n_keys = past + s_len
    k_pos = jnp.arange(n_keys)
    q_pos = past + jnp.arange(s_len)
    n_sel = min(TOPK_MAX, n_keys // 4)

    def one(args):
        qb, qib, wib, kb, vb, kib = args
        return _dsa_attend(qb[None], qib[None], wib[None], q_pos, kb[None], vb[None], kib[None], k_pos, n_sel, slopes)[0]

    o = lax.map(one, (q, qi, wi, k_all, v_all, ki_all))
    return o.reshape(bsz, s_len, D_MODEL) @ w_o, k, v, ki


def setup_inputs(seed: int = 0) -> dict:
    key = jax.random.key(seed)
    ks = jax.random.split(key, 18)
    f32 = jnp.float32
    a_len = min(N_PREV_CHUNKS * CHUNK, PAST_LEN)
    sd = D_MODEL ** -0.5
    a_col = jnp.concatenate([jnp.ones((2 * D_MODEL,), f32), jnp.full((D_MODEL,), DN_BETA, f32)])
    b_col = jnp.concatenate([jnp.ones((2 * D_MODEL,), f32), jnp.full((D_MODEL,), DN_BETA, f32),
                             jnp.ones((B_IN_WIDTH - 3 * D_MODEL,), f32)])
    return {
        'x_prompt': jax.random.normal(ks[0], (BATCH, SEQ, D_MODEL), f32),
        'x_sample': jax.random.normal(ks[1], (DEC_BATCH, DEC_SEQ, D_MODEL), f32),
        'cache_a_k': jax.random.normal(ks[2], (N_A_LAYERS, DEC_BATCH, a_len, A_HEADS, A_HEAD_DIM), f32),
        'cache_a_v': DN_BETA * jax.random.normal(ks[3], (N_A_LAYERS, DEC_BATCH, a_len, A_HEADS, A_HEAD_DIM), f32),
        'cache_b_k': jax.random.normal(ks[4], (N_B_LAYERS, DEC_BATCH, PAST_LEN, B_HEADS, B_HEAD_DIM), f32),
        'cache_b_v': DN_BETA * jax.random.normal(ks[5], (N_B_LAYERS, DEC_BATCH, PAST_LEN, B_HEADS, B_HEAD_DIM), f32),
        'cache_b_kidx': jax.random.normal(ks[6], (N_B_LAYERS, DEC_BATCH, PAST_LEN, IDX_DIM), f32),
        'w_a_qkv': jax.random.normal(ks[7], (N_A_LAYERS, D_MODEL, 3 * D_MODEL), f32) * sd * a_col,
        'w_a_o': jax.random.normal(ks[8], (N_A_LAYERS, D_MODEL, D_MODEL), f32) * (sd * DN_BETA),
        'a_rel_bias': 0.1 * jax.random.normal(ks[9], (N_A_LAYERS, A_HEADS, 2 * REL_CLIP + 1), f32),
        'w_b_in': jax.random.normal(ks[10], (N_B_LAYERS, D_MODEL, B_IN_WIDTH), f32) * sd * b_col,
        'w_b_o': jax.random.normal(ks[11], (N_B_LAYERS, D_MODEL, D_MODEL), f32) * (sd * DN_BETA),
        'w_ff1': jax.random.normal(ks[12], (DEPTH, D_MODEL, D_FF), f32) * (sd * DN_BETA),
        'w_ff2': jax.random.normal(ks[13], (DEPTH, D_FF, D_MODEL), f32) * (D_FF ** -0.5 * DN_BETA),
        'ln1_g': 1.0 + 0.02 * jax.random.normal(ks[14], (DEPTH, D_MODEL), f32),
        'ln1_b': 0.02 * jax.random.normal(ks[15], (DEPTH, D_MODEL), f32),
        'ln2_g': 1.0 + 0.02 * jax.random.normal(ks[16], (DEPTH, D_MODEL), f32),
        'ln2_b': 0.02 * jax.random.normal(ks[17], (DEPTH, D_MODEL), f32),
    }


def reference(x_prompt, x_sample, cache_a_k, cache_a_v, cache_b_k, cache_b_v, cache_b_kidx,
              w_a_qkv, w_a_o, a_rel_bias, w_b_in, w_b_o, w_ff1, w_ff2, ln1_g, ln1_b, ln2_g, ln2_b):
    slopes = _alibi_slopes(B_HEADS)
    xp, xs = x_prompt, x_sample
    ak_p, av_p, ak_s, av_s = [], [], [], []
    bk_p, bv_p, bi_p, bk_s, bv_s, bi_s = [], [], [], [], [], []
    for i in range(DEPTH):
        j = i // 2
        if i % 2 == 0:
            mp, kp, vp = mixer_a_prompt(xp, w_a_qkv[j], w_a_o[j], a_rel_bias[j])
            ms, k_s, v_s = mixer_a_sample(xs, cache_a_k[j], cache_a_v[j], w_a_qkv[j], w_a_o[j], a_rel_bias[j])
            ak_p.append(kp); av_p.append(vp); ak_s.append(k_s); av_s.append(v_s)
        else:
            mp, kp, vp, kip = mixer_b_prompt(xp, w_b_in[j], w_b_o[j], slopes)
            ms, k_s, v_s, kis = mixer_b_sample(xs, cache_b_k[j], cache_b_v[j], cache_b_kidx[j], w_b_in[j], w_b_o[j], slopes)
            bk_p.append(kp); bv_p.append(vp); bi_p.append(kip)
            bk_s.append(k_s); bv_s.append(v_s); bi_s.append(kis)
        xp = _layer_norm(DN_ALPHA * xp + mp, ln1_g[i], ln1_b[i])
        xs = _layer_norm(DN_ALPHA * xs + ms, ln1_g[i], ln1_b[i])
        xp = _layer_norm(DN_ALPHA * xp + _ffn(xp, w_ff1[i], w_ff2[i]), ln2_g[i], ln2_b[i])
        xs = _layer_norm(DN_ALPHA * xs + _ffn(xs, w_ff1[i], w_ff2[i]), ln2_g[i], ln2_b[i])
    return (xp, xs,
            jnp.stack(ak_p), jnp.stack(av_p), jnp.stack(bk_p), jnp.stack(bv_p), jnp.stack(bi_p),
            jnp.stack(ak_s), jnp.stack(av_s), jnp.stack(bk_s), jnp.stack(bv_s), jnp.stack(bi_s))
```

```python
import functools
import math

import numpy as np
import jax
import jax.numpy as jnp
from jax import lax
from jax.experimental import pallas as pl
from jax.experimental.pallas import tpu as pltpu

D_MODEL = 1024
N_HEADS = 16
HEAD_DIM = 64
PAIR = 2 * HEAD_DIM
N_PAIRS = N_HEADS // 2
CHUNK = 64
N_PREV_CHUNKS = 8
REL_CLIP = 128
IDX_HEADS = 8
IDX_DIM = 64
IDX_WIDTH = IDX_HEADS * IDX_DIM + IDX_DIM + IDX_HEADS
IDX_PAD = 640
TOPK_MAX = 256
LN_EPS = 1e-5

NEG = -1e30
M_INIT = -3e38
KEY_NEG_INF = -2139095041
INT_MIN = -2147483648
INT_MAX = 2147483647
V7X_VMEM_LIMIT = 60 * 1024 * 1024

F32 = jnp.float32
BF16 = jnp.bfloat16


def _params(sem):
    return pltpu.CompilerParams(dimension_semantics=sem, vmem_limit_bytes=V7X_VMEM_LIMIT)


def _row_tile(m, cap):
    t = cap
    while m % t:
        t //= 2
    return t


def _const_spec(shape):
    nd = len(shape)
    return pl.BlockSpec(shape, lambda *_: (0,) * nd, pipeline_mode=pl.Buffered(1))


def _proj_kernel(x_ref, w_ref, *out_refs, groups):
    xb = x_ref[...].astype(BF16)
    for col, width, outs in groups:
        r = jnp.dot(xb, w_ref[:, col:col + width], preferred_element_type=F32)
        for idx, scale in outs:
            o = out_refs[idx]
            o[...] = (r * scale if scale != 1.0 else r).astype(o.dtype)


def _proj(x, w, groups, out_defs):
    m = x.shape[0]
    tm = _row_tile(m, 512)
    out_shape = [jax.ShapeDtypeStruct((m, wd), dt) for wd, dt in out_defs]
    out_specs = [pl.BlockSpec((tm, wd), lambda i: (i, 0)) for wd, _ in out_defs]
    return pl.pallas_call(
        functools.partial(_proj_kernel, groups=groups),
        grid=(m // tm,),
        in_specs=[pl.BlockSpec((tm, D_MODEL), lambda i: (i, 0)), _const_spec(w.shape)],
        out_specs=out_specs,
        out_shape=out_shape,
        compiler_params=_params(("parallel",)),
    )(x, w)


def _deepnorm_ln(x, r, g, b, alpha):
    y = alpha * x + r
    mu = jnp.mean(y, axis=-1, keepdims=True)
    yc = y - mu
    var = jnp.mean(yc * yc, axis=-1, keepdims=True)
    return yc * lax.rsqrt(var + LN_EPS) * g + b


def _oproj_ln_kernel(o_ref, w_ref, x_ref, g_ref, b_ref, out_ref, *, alpha):
    r = jnp.dot(o_ref[...], w_ref[...], preferred_element_type=F32)
    out_ref[...] = _deepnorm_ln(x_ref[...], r, g_ref[...], b_ref[...], alpha)


def _oproj_ln(o, w, x, g, b, alpha):
    m = x.shape[0]
    tm = _row_tile(m, 512)
    row = pl.BlockSpec((tm, D_MODEL), lambda i: (i, 0))
    return pl.pallas_call(
        functools.partial(_oproj_ln_kernel, alpha=alpha),
        grid=(m // tm,),
        in_specs=[row, _const_spec(w.shape), row, _const_spec((1, D_MODEL)), _const_spec((1, D_MODEL))],
        out_specs=row,
        out_shape=jax.ShapeDtypeStruct((m, D_MODEL), F32),
        compiler_params=_params(("parallel",)),
    )(o, w, x, g.reshape(1, D_MODEL), b.reshape(1, D_MODEL))


def _ffn_ln_kernel(x_ref, w1_ref, w2_ref, g_ref, b_ref, out_ref, acc_ref, *, alpha):
    f = pl.program_id(1)

    @pl.when(f == 0)
    def _():
        acc_ref[...] = jnp.zeros_like(acc_ref)

    h = jnp.dot(x_ref[...].astype(BF16), w1_ref[...], preferred_element_type=F32)
    h = jnp.maximum(h, 0.0)
    acc_ref[...] += jnp.dot((h * h).astype(BF16), w2_ref[...], preferred_element_type=F32)

    @pl.when(f == pl.num_programs(1) - 1)
    def _():
        out_ref[...] = _deepnorm_ln(x_ref[...], acc_ref[...], g_ref[...], b_ref[...], alpha)


def _ffn_ln(x, w1, w2, g, b, alpha):
    m = x.shape[0]
    d_ff = w1.shape[1]
    tm = _row_tile(m, 1024)
    tf = _row_tile(d_ff, 1024)
    row = pl.BlockSpec((tm, D_MODEL), lambda i, f: (i, 0))
    return pl.pallas_call(
        functools.partial(_ffn_ln_kernel, alpha=alpha),
        grid=(m // tm, d_ff // tf),
        in_specs=[row,
                  pl.BlockSpec((D_MODEL, tf), lambda i, f: (0, f)),
                  pl.BlockSpec((tf, D_MODEL), lambda i, f: (f, 0)),
                  _const_spec((1, D_MODEL)), _const_spec((1, D_MODEL))],
        out_specs=row,
        out_shape=jax.ShapeDtypeStruct((m, D_MODEL), F32),
        scratch_shapes=[pltpu.VMEM((tm, D_MODEL), F32)],
        compiler_params=_params(("parallel", "arbitrary")),
    )(x, w1, w2, g.reshape(1, D_MODEL), b.reshape(1, D_MODEL))


def _band_kernel(*refs, seg_counts, n_lead_invalid_fn, tq):
    n_seg = sum(seg_counts)
    q_ref = refs[0]
    k_refs = refs[1:1 + n_seg]
    v_refs = refs[1 + n_seg:1 + 2 * n_seg]
    bias_refs = refs[1 + 2 * n_seg:1 + 2 * n_seg + len(seg_counts)]
    o_ref = refs[-1]

    lane = lax.broadcasted_iota(jnp.int32, (tq, PAIR), 1)
    upper = lane >= HEAD_DIM
    for hp in range(N_PAIRS):
        cols = slice(hp * PAIR, (hp + 1) * PAIR)
        qp = q_ref[0, :, cols]
        kg, vg = [], []
        s0 = 0
        for cnt in seg_counts:
            ks = [k_refs[s0 + t][0, :, cols] for t in range(cnt)]
            vs = [v_refs[s0 + t][0, :, cols] for t in range(cnt)]
            kg.append(ks[0] if cnt == 1 else jnp.concatenate(ks, axis=0))
            vg.append(vs[0] if cnt == 1 else jnp.concatenate(vs, axis=0))
            s0 += cnt
        outs = []
        for par in range(2):
            h = 2 * hp + par
            qm = jnp.where(upper == (par == 1), qp, jnp.zeros_like(qp))
            s = []
            for g, kmat in enumerate(kg):
                sg = lax.dot_general(qm, kmat, (((1,), (1,)), ((), ())), preferred_element_type=F32)
                sg = sg + bias_refs[g][h]
                if g == 0 and n_lead_invalid_fn is not None:
                    col = lax.broadcasted_iota(jnp.int32, sg.shape, 1)
                    sg = jnp.where(col >= n_lead_invalid_fn(pl.program_id(1)), sg, NEG)
                s.append(sg)
            m = functools.reduce(jnp.maximum, [jnp.max(sg, axis=1, keepdims=True) for sg in s])
            p = [jnp.exp(sg - m) for sg in s]
            l = functools.reduce(jnp.add, [jnp.sum(pg, axis=1, keepdims=True) for pg in p])
            o = functools.reduce(jnp.add, [jnp.dot(pg.astype(BF16), vmat, preferred_element_type=F32)
                                           for pg, vmat in zip(p, vg)])
            outs.append(o / l)
        o_ref[0, :, cols] = jnp.where(upper, outs[1], outs[0]).astype(o_ref.dtype)


def _band_bias(tab, q_pos, k_pos):
    rel = q_pos[:, None] - k_pos[None, :]
    bias = tab[:, jnp.clip(rel, -REL_CLIP, REL_CLIP) + REL_CLIP].astype(F32)
    qch = q_pos[:, None] // CHUNK
    kch = k_pos[None, :] // CHUNK
    valid = (kch <= qch) & (kch >= qch - N_PREV_CHUNKS)
    return jnp.where(valid[None], bias, NEG)


def _band_prompt(q, k, v, tab):
    bsz, t, _ = q.shape
    tq = 2 * CHUNK
    n_seg = (N_PREV_CHUNKS * CHUNK) // tq + 1
    lead = (n_seg - 1) * tq
    bias = _band_bias(tab, lead + jnp.arange(tq), jnp.arange(n_seg * tq))

    def kv_spec(s):
        return pl.BlockSpec((1, tq, D_MODEL), lambda b, i: (b, jnp.maximum(i - (n_seg - 1) + s, 0), 0))

    qspec = pl.BlockSpec((1, tq, D_MODEL), lambda b, i: (b, i, 0))
    kern = functools.partial(_band_kernel, seg_counts=(n_seg,), tq=tq,
                             n_lead_invalid_fn=lambda i: lead - i * tq)
    return pl.pallas_call(
        kern,
        grid=(bsz, t // tq),
        in_specs=[qspec] + [kv_spec(s) for s in range(n_seg)] * 2 + [_const_spec(bias.shape)],
        out_specs=qspec,
        out_shape=jax.ShapeDtypeStruct(q.shape, BF16),
        compiler_params=_params(("parallel", "parallel")),
    )(q, *([k] * n_seg), *([v] * n_seg), bias)


def _band_sample(q, k_new, v_new, cache_k, cache_v, tab, past):
    bsz, s_len, _ = q.shape
    a_len = cache_k.shape[1]
    q_pos = past + jnp.arange(s_len)
    bias_c = _band_bias(tab, q_pos, past - a_len + jnp.arange(a_len))
    bias_n = _band_bias(tab, q_pos, q_pos)

    def spec(n):
        return pl.BlockSpec((1, n, D_MODEL), lambda b, i: (b, 0, 0))

    kern = functools.partial(_band_kernel, seg_counts=(1, 1), tq=s_len, n_lead_invalid_fn=None)
    return pl.pallas_call(
        kern,
        grid=(bsz, 1),
        in_specs=[spec(s_len), spec(a_len), spec(s_len), spec(a_len), spec(s_len),
                  _const_spec(bias_c.shape), _const_spec(bias_n.shape)],
        out_specs=spec(s_len),
        out_shape=jax.ShapeDtypeStruct(q.shape, BF16),
        compiler_params=_params(("parallel", "arbitrary")),
    )(q, cache_k, k_new, cache_v, v_new, bias_c, bias_n)


def _dsa_kernel(q_ref, qi_ref, wi_ref, k_ref, v_ref, kit_ref, o_ref,
                key_scr, acc_scr, m_scr, l_scr, j_scr,
                *, tq, tk, q_off, n_keys, n_sel, slopes, idx_scale):
    i = pl.program_id(1)
    q_pos0 = q_off + i * tq
    last_chunk = (q_pos0 + tq - 1) // CHUNK
    n_adm = jnp.minimum((last_chunk + 1) * CHUNK, n_keys)
    n_tiles = (n_adm + tk - 1) // tk

    row = lax.broadcasted_iota(jnp.int32, (tq, 1), 0)
    q_pos = q_pos0 + row
    q_chunk = q_pos // CHUNK

    def tile_start(j):
        return pl.multiple_of(j * tk, tk)

    def key_pos(j):
        return j * tk + lax.broadcasted_iota(jnp.int32, (1, tk), 1)

    wi = wi_ref[0] * idx_scale

    def score_tile(j, carry):
        kit = kit_ref[0, :, pl.ds(tile_start(j), tk)]
        acc = jnp.zeros((tq, tk), F32)
        for h in range(IDX_HEADS):
            lg = jnp.dot(qi_ref[0, :, h * IDX_DIM:(h + 1) * IDX_DIM], kit, preferred_element_type=F32)
            acc = acc + wi[:, h:h + 1] * jnp.maximum(lg, 0.0)
        acc = jnp.where(acc == 0.0, 0.0, acc)
        bits = pltpu.bitcast(acc, jnp.int32)
        key = bits ^ ((bits >> 31) & INT_MAX)
        kp = key_pos(j)
        adm = ((kp // CHUNK) <= q_chunk) & (kp < n_keys)
        key_scr[:, pl.ds(tile_start(j), tk)] = jnp.where(adm, key, KEY_NEG_INF)
        return carry

    lax.fori_loop(0, n_tiles, score_tile, 0)

    def count(pred):
        def body(j, cnt):
            kt = key_scr[:, pl.ds(tile_start(j), tk)]
            hit = pred(kt, key_pos(j)).astype(jnp.int32)
            for c in range(tk // 128):
                cnt = cnt + hit[:, c * 128:(c + 1) * 128]
            return cnt
        cnt = lax.fori_loop(0, n_tiles, body, jnp.zeros((tq, 128), jnp.int32))
        return jnp.sum(cnt, axis=1, keepdims=True)

    def thr_step(b, carry):
        thr, cnt_thr = carry
        cand = thr + lax.shift_left(jnp.int32(1), 31 - b)
        c = count(lambda kt, kp: kt >= cand)
        ok = c >= n_sel
        return jnp.where(ok, cand, thr), jnp.where(ok, c, cnt_thr)

    thr, cnt_thr = lax.fori_loop(
        0, 32, thr_step, (jnp.full((tq, 1), INT_MIN, jnp.int32), jnp.zeros((tq, 1), jnp.int32)))

    tie = cnt_thr > n_sel
    j_scr[...] = jnp.full((tq, 1), INT_MAX, jnp.int32)

    @pl.when(jnp.max(tie.astype(jnp.int32)) > 0)
    def _():
        need = n_sel - count(lambda kt, kp: kt > thr)

        n_bits = int(key_scr.shape[1]).bit_length()

        def pos_step(b, cut):
            cand = cut + lax.shift_left(jnp.int32(1), n_bits - 1 - b)
            c = count(lambda kt, kp: (kt == thr) & (kp < cand))
            return jnp.where(c < need, cand, cut)

        cut = lax.fori_loop(0, n_bits, pos_step, jnp.zeros((tq, 1), jnp.int32))
        j_scr[...] = jnp.where(tie, cut, INT_MAX)

    cut = j_scr[...]

    m_scr[...] = jnp.full(m_scr.shape, M_INIT, F32)
    l_scr[...] = jnp.zeros(l_scr.shape, F32)
    acc_scr[...] = jnp.zeros(acc_scr.shape, F32)
    lane = lax.broadcasted_iota(jnp.int32, (tq, PAIR), 1)
    upper = lane >= HEAD_DIM

    def attend_tile(j, carry):
        start = tile_start(j)
        kt = key_scr[:, pl.ds(start, tk)]
        kp = key_pos(j)
        sel = ((kt > thr) | ((kt == thr) & (kp <= cut))) & (kt > KEY_NEG_INF)
        mb = jnp.where(sel, 0.0, NEG)
        dist = jnp.abs(q_pos - kp).astype(F32)
        for hp in range(N_PAIRS):
            cols = slice(hp * PAIR, (hp + 1) * PAIR)
            qp = q_ref[0, :, cols]
            kmat = k_ref[0, pl.ds(start, tk), cols]
            vmat = v_ref[0, pl.ds(start, tk), cols]
            acc_old = acc_scr[:, cols]
            new = []
            for par in range(2):
                h = 2 * hp + par
                qm = jnp.where(upper == (par == 1), qp, jnp.zeros_like(qp))
                s = lax.dot_general(qm, kmat, (((1,), (1,)), ((), ())), preferred_element_type=F32)
                s = s - slopes[h] * dist + mb
                m_prev = m_scr[h]
                m_new = jnp.maximum(m_prev, jnp.max(s, axis=1, keepdims=True))
                alpha = jnp.exp(m_prev - m_new)
                p = jnp.exp(s - m_new)
                l_scr[h] = alpha * l_scr[h] + jnp.sum(p, axis=1, keepdims=True)
                m_scr[h] = m_new
                pv = jnp.dot(p.astype(BF16), vmat, preferred_element_type=F32)
                new.append(alpha * acc_old + pv)
            acc_scr[:, cols] = jnp.where(upper, new[1], new[0])
        return carry

    lax.fori_loop(0, n_tiles, attend_tile, 0)

    for hp in range(N_PAIRS):
        cols = slice(hp * PAIR, (hp + 1) * PAIR)
        inv = jnp.where(upper, 1.0 / l_scr[2 * hp + 1], 1.0 / l_scr[2 * hp])
        o_ref[0, :, cols] = (acc_scr[:, cols] * inv).astype(o_ref.dtype)


def _dsa(q, qi, wi, k, v, kit, *, tq, tk, q_off, n_keys, n_sel, resident):
    bsz, n_q, _ = q.shape
    l_pad = k.shape[1]
    slopes = tuple(float(np.float32(2.0 ** (-8.0 * (h + 1) / N_HEADS))) for h in range(N_HEADS))
    idx_scale = float(IDX_DIM ** -0.5 * IDX_HEADS ** -0.5)
    mode = dict(pipeline_mode=pl.Buffered(1)) if resident else {}

    def qspec(w):
        return pl.BlockSpec((1, tq, w), lambda b, i: (b, i, 0))

    kern = functools.partial(_dsa_kernel, tq=tq, tk=tk, q_off=q_off, n_keys=n_keys, n_sel=n_sel,
                             slopes=slopes, idx_scale=idx_scale)
    return pl.pallas_call(
        kern,
        grid=(bsz, n_q // tq),
        in_specs=[qspec(D_MODEL), qspec(IDX_HEADS * IDX_DIM), qspec(IDX_HEADS),
                  pl.BlockSpec((1, l_pad, D_MODEL), lambda b, i: (b, 0, 0), **mode),
                  pl.BlockSpec((1, l_pad, D_MODEL), lambda b, i: (b, 0, 0), **mode),
                  pl.BlockSpec((1, IDX_DIM, l_pad), lambda b, i: (b, 0, 0), **mode)],
        out_specs=qspec(D_MODEL),
        out_shape=jax.ShapeDtypeStruct(q.shape, BF16),
        scratch_shapes=[pltpu.VMEM((tq, l_pad), jnp.int32),
                        pltpu.VMEM((tq, D_MODEL), F32),
                        pltpu.VMEM((N_HEADS, tq, 1), F32),
                        pltpu.VMEM((N_HEADS, tq, 1), F32),
                        pltpu.VMEM((tq, 1), jnp.int32)],
        compiler_params=_params(("parallel", "arbitrary")),
    )(q, qi, wi, k, v, kit)


def kernel(x_prompt, x_sample, cache_a_k, cache_a_v, cache_b_k, cache_b_v, cache_b_kidx,
           w_a_qkv, w_a_o, a_rel_bias, w_b_in, w_b_o, w_ff1, w_ff2, ln1_g, ln1_b, ln2_g, ln2_b):
    bsz, seq, _ = x_prompt.shape
    dbsz, dseq, _ = x_sample.shape
    depth = w_ff1.shape[0]
    past = cache_b_k.shape[2]
    alpha = float((2.0 * depth) ** 0.25)
    n_p = bsz * seq
    q_scale = float(HEAD_DIM ** -0.5)

    x = jnp.concatenate([x_prompt.reshape(n_p, D_MODEL), x_sample.reshape(dbsz * dseq, D_MODEL)], axis=0)

    def split(a, width):
        return a[:n_p].reshape(bsz, seq, width), a[n_p:].reshape(dbsz, dseq, width)

    def heads(a):
        return a.reshape(a.shape[:-1] + (N_HEADS, HEAD_DIM))

    qkv_groups = [(0, D_MODEL, [(0, q_scale)]),
                  (D_MODEL, D_MODEL, [(1, 1.0), (2, 1.0)]),
                  (2 * D_MODEL, D_MODEL, [(3, 1.0), (4, 1.0)])]
    qkv_defs = [(D_MODEL, BF16), (D_MODEL, F32), (D_MODEL, BF16), (D_MODEL, F32), (D_MODEL, BF16)]

    ak_p, av_p, ak_s, av_s = [], [], [], []
    bk_p, bv_p, bi_p, bk_s, bv_s, bi_s = [], [], [], [], [], []
    for i in range(depth):
        j = i // 2
        if i % 2 == 0:
            q, k32, k16, v32, v16 = _proj(x, w_a_qkv[j].astype(BF16), qkv_groups, qkv_defs)
            q_p, q_s = split(q, D_MODEL)
            k_p, k_s = split(k16, D_MODEL)
            v_p, v_s = split(v16, D_MODEL)
            o_p = _band_prompt(q_p, k_p, v_p, a_rel_bias[j])
            a_len = cache_a_k.shape[2]
            ck = cache_a_k[j].reshape(dbsz, a_len, D_MODEL).astype(BF16)
            cv = cache_a_v[j].reshape(dbsz, a_len, D_MODEL).astype(BF16)
            o_s = _band_sample(q_s, k_s, v_s, ck, cv, a_rel_bias[j], past)
            k32_p, k32_s = split(k32, D_MODEL)
            v32_p, v32_s = split(v32, D_MODEL)
            n_keep = min(N_PREV_CHUNKS * CHUNK, seq)
            ak_p.append(heads(k32_p[:, seq - n_keep:])); av_p.append(heads(v32_p[:, seq - n_keep:]))
            ak_s.append(heads(k32_s)); av_s.append(heads(v32_s))
            w_o = w_a_o[j]
        else:
            w_in = w_b_in[j]
            w_cat = jnp.concatenate(
                [w_in[:, :3 * D_MODEL], jnp.pad(w_in[:, 3 * D_MODEL:], ((0, 0), (0, IDX_PAD - IDX_WIDTH)))],
                axis=1).astype(BF16)
            groups = qkv_groups + [(3 * D_MODEL, IDX_PAD, [(5, 1.0)])]
            q, k32, k16, v32, v16, idx = _proj(x, w_cat, groups, qkv_defs + [(IDX_PAD, F32)])
            n_qi = IDX_HEADS * IDX_DIM
            qi_p, qi_s = split(idx[:, :n_qi].astype(BF16), n_qi)
            ki_p, ki_s = split(idx[:, n_qi:n_qi + IDX_DIM], IDX_DIM)
            wi_p, wi_s = split(idx[:, n_qi + IDX_DIM:IDX_WIDTH], IDX_HEADS)
            q_p, q_s = split(q, D_MODEL)
            k_p, k_s = split(k16, D_MODEL)
            v_p, v_s = split(v16, D_MODEL)

            tk_p = _row_tile(seq, 512)
            o_p = _dsa(q_p, qi_p, wi_p, k_p, v_p, jnp.swapaxes(ki_p, 1, 2).astype(BF16),
                       tq=2 * CHUNK, tk=tk_p, q_off=0, n_keys=seq, n_sel=min(TOPK_MAX, seq // 4),
                       resident=True)

            n_keys = past + dseq
            l_pad = -(-n_keys // 128) * 128
            padk = ((0, 0), (0, l_pad - n_keys), (0, 0))
            k_all = jnp.pad(jnp.concatenate(
                [cache_b_k[j].reshape(dbsz, past, D_MODEL).astype(BF16), k_s], axis=1), padk)
            v_all = jnp.pad(jnp.concatenate(
                [cache_b_v[j].reshape(dbsz, past, D_MODEL).astype(BF16), v_s], axis=1), padk)
            ki_all = jnp.pad(jnp.concatenate([cache_b_kidx[j], ki_s], axis=1), padk)
            o_s = _dsa(q_s, qi_s, wi_s, k_all, v_all, jnp.swapaxes(ki_all, 1, 2).astype(BF16),
                       tq=dseq, tk=128, q_off=past, n_keys=n_keys, n_sel=min(TOPK_MAX, n_keys // 4),
                       resident=False)

            k32_p, k32_s = split(k32, D_MODEL)
            v32_p, v32_s = split(v32, D_MODEL)
            bk_p.append(heads(k32_p)); bv_p.append(heads(v32_p)); bi_p.append(ki_p)
            bk_s.append(heads(k32_s)); bv_s.append(heads(v32_s)); bi_s.append(ki_s)
            w_o = w_b_o[j]

        o = jnp.concatenate([o_p.reshape(n_p, D_MODEL), o_s.reshape(dbsz * dseq, D_MODEL)], axis=0)
        x = _oproj_ln(o, w_o.astype(BF16), x, ln1_g[i], ln1_b[i], alpha)
        x = _ffn_ln(x, w_ff1[i].astype(BF16), w_ff2[i].astype(BF16), ln2_g[i], ln2_b[i], alpha)

    y_p, y_s = split(x, D_MODEL)
    return (y_p, y_s,
            jnp.stack(ak_p), jnp.stack(av_p), jnp.stack(bk_p), jnp.stack(bv_p), jnp.stack(bi_p),
            jnp.stack(ak_s), jnp.stack(av_s), jnp.stack(bk_s), jnp.stack(bv_s), jnp.stack(bi_s))
```

```python
import functools
import math

import numpy as np
import jax
import jax.numpy as jnp
from jax import lax
from jax.experimental import pallas as pl
from jax.experimental.pallas import tpu as pltpu

D_MODEL = 1024
N_HEADS = 16
HEAD_DIM = 64
PAIR = 2 * HEAD_DIM
N_PAIRS = N_HEADS // 2
CHUNK = 64
N_PREV_CHUNKS = 8
REL_CLIP = 128
IDX_HEADS = 8
IDX_DIM = 64
IDX_WIDTH = IDX_HEADS * IDX_DIM + IDX_DIM + IDX_HEADS
IDX_PAD = 640
TOPK_MAX = 256
LN_EPS = 1e-5
LOG2E = math.log2(math.e)

NEG = -1e30
M_INIT = -3e38
SCORE_LOOKAHEAD = 2
ONES_ROWS = 16
VT_ROWS = PAIR + ONES_ROWS
KEY_NEG_INF = -2139095041
INT_MIN = -2147483648
INT_MAX = 2147483647
V7X_VMEM_LIMIT = 60 * 1024 * 1024

F32 = jnp.float32
BF16 = jnp.bfloat16


def _params(sem, flags=None):
    return pltpu.CompilerParams(dimension_semantics=sem, vmem_limit_bytes=V7X_VMEM_LIMIT, flags=flags)


def _row_tile(m, cap):
    t = cap
    while m % t:
        t //= 2
    return t


def _const_spec(shape):
    nd = len(shape)
    return pl.BlockSpec(shape, lambda *_: (0,) * nd, pipeline_mode=pl.Buffered(1))


def _proj_kernel(x_ref, w_ref, *out_refs, groups):
    xb = x_ref[...].astype(BF16)
    for col, width, outs in groups:
        r = jnp.dot(xb, w_ref[:, col:col + width], preferred_element_type=F32)
        for idx, scale, layout in outs:
            o = out_refs[idx]
            v = r * scale if scale != 1.0 else r
            if layout == "rows":
                o[...] = v.astype(o.dtype)
            elif layout == "cols":
                o[...] = v.T.astype(o.dtype)
            else:
                vt = v.T.astype(o.dtype)
                for hp in range(N_PAIRS):
                    o[hp * VT_ROWS:hp * VT_ROWS + PAIR, :] = vt[hp * PAIR:(hp + 1) * PAIR]
                    o[hp * VT_ROWS + PAIR:(hp + 1) * VT_ROWS, :] = jnp.ones((ONES_ROWS, vt.shape[1]), o.dtype)


def _proj(x, w, groups, out_defs):
    m = x.shape[0]
    tm = _row_tile(m, 512)
    out_shape = [jax.ShapeDtypeStruct((m, wd) if lay == "rows" else (wd, m), dt) for wd, dt, lay in out_defs]
    out_specs = [pl.BlockSpec((tm, wd), lambda i: (i, 0)) if lay == "rows" else pl.BlockSpec((wd, tm), lambda i: (0, i))
                 for wd, _, lay in out_defs]
    return pl.pallas_call(
        functools.partial(_proj_kernel, groups=groups),
        grid=(m // tm,),
        in_specs=[pl.BlockSpec((tm, D_MODEL), lambda i: (i, 0)), _const_spec(w.shape)],
        out_specs=out_specs,
        out_shape=out_shape,
        compiler_params=_params(("parallel",)),
    )(x, w)


def _deepnorm_ln(x, r, g, b, alpha):
    y = alpha * x + r
    mu = jnp.mean(y, axis=-1, keepdims=True)
    yc = y - mu
    var = jnp.mean(yc * yc, axis=-1, keepdims=True)
    return yc * lax.rsqrt(var + LN_EPS) * g + b


def _oproj_ln_kernel(o_ref, w_ref, x_ref, g_ref, b_ref, out_ref, *, alpha):
    r = jnp.dot(o_ref[...], w_ref[...], preferred_element_type=F32)
    out_ref[...] = _deepnorm_ln(x_ref[...], r, g_ref[...], b_ref[...], alpha)


def _oproj_ln(o, w, x, g, b, alpha):
    m = x.shape[0]
    tm = _row_tile(m, 512)
    row = pl.BlockSpec((tm, D_MODEL), lambda i: (i, 0))
    return pl.pallas_call(
        functools.partial(_oproj_ln_kernel, alpha=alpha),
        grid=(m // tm,),
        in_specs=[row, _const_spec(w.shape), row, _const_spec((1, D_MODEL)), _const_spec((1, D_MODEL))],
        out_specs=row,
        out_shape=jax.ShapeDtypeStruct((m, D_MODEL), F32),
        compiler_params=_params(("parallel",)),
    )(o, w, x, g.reshape(1, D_MODEL), b.reshape(1, D_MODEL))


def _ffn_ln_kernel(x_ref, w1_ref, w2_ref, g_ref, b_ref, out_ref, acc_ref, *, alpha):
    f = pl.program_id(1)

    @pl.when(f == 0)
    def _():
        acc_ref[...] = jnp.zeros_like(acc_ref)

    h = jnp.dot(x_ref[...].astype(BF16), w1_ref[...], preferred_element_type=F32)
    h = jnp.maximum(h, 0.0)
    acc_ref[...] += jnp.dot((h * h).astype(BF16), w2_ref[...], preferred_element_type=F32)

    @pl.when(f == pl.num_programs(1) - 1)
    def _():
        out_ref[...] = _deepnorm_ln(x_ref[...], acc_ref[...], g_ref[...], b_ref[...], alpha)


def _ffn_ln(x, w1, w2, g, b, alpha):
    m = x.shape[0]
    d_ff = w1.shape[1]
    tm = _row_tile(m, 1024)
    tf = _row_tile(d_ff, 1024)
    row = pl.BlockSpec((tm, D_MODEL), lambda i, f: (i, 0))
    return pl.pallas_call(
        functools.partial(_ffn_ln_kernel, alpha=alpha),
        grid=(m // tm, d_ff // tf),
        in_specs=[row,
                  pl.BlockSpec((D_MODEL, tf), lambda i, f: (0, f)),
                  pl.BlockSpec((tf, D_MODEL), lambda i, f: (f, 0)),
                  _const_spec((1, D_MODEL)), _const_spec((1, D_MODEL))],
        out_specs=row,
        out_shape=jax.ShapeDtypeStruct((m, D_MODEL), F32),
        scratch_shapes=[pltpu.VMEM((tm, D_MODEL), F32)],
        compiler_params=_params(("parallel", "arbitrary")),
    )(x, w1, w2, g.reshape(1, D_MODEL), b.reshape(1, D_MODEL))


def _band_kernel(*refs, seg_counts, n_lead_invalid_fn, tq):
    n_seg = sum(seg_counts)
    q_ref = refs[0]
    k_refs = refs[1:1 + n_seg]
    v_refs = refs[1 + n_seg:1 + 2 * n_seg]
    bias_refs = refs[1 + 2 * n_seg:1 + 2 * n_seg + len(seg_counts)]
    o_ref = refs[-1]

    lane = lax.broadcasted_iota(jnp.int32, (tq, PAIR), 1)
    upper = lane >= HEAD_DIM
    for hp in range(N_PAIRS):
        cols = slice(hp * PAIR, (hp + 1) * PAIR)
        qp = q_ref[0, :, cols]
        kg, vg = [], []
        s0 = 0
        for cnt in seg_counts:
            ks = [k_refs[s0 + t][0, :, cols] for t in range(cnt)]
            vs = [v_refs[s0 + t][0, :, cols] for t in range(cnt)]
            kg.append(ks[0] if cnt == 1 else jnp.concatenate(ks, axis=0))
            vg.append(vs[0] if cnt == 1 else jnp.concatenate(vs, axis=0))
            s0 += cnt
        outs = []
        for par in range(2):
            h = 2 * hp + par
            qm = jnp.where(upper == (par == 1), qp, jnp.zeros_like(qp))
            s = []
            for g, kmat in enumerate(kg):
                sg = lax.dot_general(qm, kmat, (((1,), (1,)), ((), ())), preferred_element_type=F32)
                sg = sg + bias_refs[g][h]
                if g == 0 and n_lead_invalid_fn is not None:
                    col = lax.broadcasted_iota(jnp.int32, sg.shape, 1)
                    sg = jnp.where(col >= n_lead_invalid_fn(pl.program_id(1)), sg, NEG)
                s.append(sg)
            m = functools.reduce(jnp.maximum, [jnp.max(sg, axis=1, keepdims=True) for sg in s])
            p = [jnp.exp2(sg - m) for sg in s]
            l = functools.reduce(jnp.add, [jnp.sum(pg, axis=1, keepdims=True) for pg in p])
            o = functools.reduce(jnp.add, [jnp.dot(pg.astype(BF16), vmat, preferred_element_type=F32)
                                           for pg, vmat in zip(p, vg)])
            outs.append(o / l)
        o_ref[0, :, cols] = jnp.where(upper, outs[1], outs[0]).astype(o_ref.dtype)


def _band_bias(tab, q_pos, k_pos):
    rel = q_pos[:, None] - k_pos[None, :]
    bias = tab[:, jnp.clip(rel, -REL_CLIP, REL_CLIP) + REL_CLIP].astype(F32) * LOG2E
    qch = q_pos[:, None] // CHUNK
    kch = k_pos[None, :] // CHUNK
    valid = (kch <= qch) & (kch >= qch - N_PREV_CHUNKS)
    return jnp.where(valid[None], bias, NEG)


def _band_prompt(q, k, v, tab):
    bsz, t, _ = q.shape
    tq = 2 * CHUNK
    n_seg = (N_PREV_CHUNKS * CHUNK) // tq + 1
    lead = (n_seg - 1) * tq
    bias = _band_bias(tab, lead + jnp.arange(tq), jnp.arange(n_seg * tq))

    def kv_spec(s):
        return pl.BlockSpec((1, tq, D_MODEL), lambda b, i: (b, jnp.maximum(i - (n_seg - 1) + s, 0), 0))

    qspec = pl.BlockSpec((1, tq, D_MODEL), lambda b, i: (b, i, 0))
    kern = functools.partial(_band_kernel, seg_counts=(n_seg,), tq=tq,
                             n_lead_invalid_fn=lambda i: lead - i * tq)
    return pl.pallas_call(
        kern,
        grid=(bsz, t // tq),
        in_specs=[qspec] + [kv_spec(s) for s in range(n_seg)] * 2 + [_const_spec(bias.shape)],
        out_specs=qspec,
        out_shape=jax.ShapeDtypeStruct(q.shape, BF16),
        compiler_params=_params(("parallel", "parallel")),
    )(q, *([k] * n_seg), *([v] * n_seg), bias)


def _band_sample(q, k_new, v_new, cache_k, cache_v, tab, past):
    bsz, s_len, _ = q.shape
    a_len = cache_k.shape[1]
    q_pos = past + jnp.arange(s_len)
    bias_c = _band_bias(tab, q_pos, past - a_len + jnp.arange(a_len))
    bias_n = _band_bias(tab, q_pos, q_pos)

    def spec(n):
        return pl.BlockSpec((1, n, D_MODEL), lambda b, i: (b, 0, 0))

    kern = functools.partial(_band_kernel, seg_counts=(1, 1), tq=s_len, n_lead_invalid_fn=None)
    return pl.pallas_call(
        kern,
        grid=(bsz, 1),
        in_specs=[spec(s_len), spec(a_len), spec(s_len), spec(a_len), spec(s_len),
                  _const_spec(bias_c.shape), _const_spec(bias_n.shape)],
        out_specs=spec(s_len),
        out_shape=jax.ShapeDtypeStruct(q.shape, BF16),
        compiler_params=_params(("parallel", "arbitrary")),
    )(q, cache_k, k_new, cache_v, v_new, bias_c, bias_n)


def _dsa_kernel(qt_ref, qit_ref, wit_ref, k_ref, vt_ref, ki_ref, o_ref,
                key_scr, cut_scr, q2t_scr, qi2t_scr, m_scr, l_scr, acc_scr, *s_scrs,
                tq, tk, n_q_valid, q_off, n_keys, n_sel, slopes, idx_scale):
    i = pl.program_id(1)
    q_pos0 = q_off + i * tq
    last_chunk = (q_pos0 + tq - 1) // CHUNK
    n_adm = jnp.minimum((last_chunk + 1) * CHUNK, n_keys)
    n_tiles = (n_adm + tk - 1) // tk

    q_lane = lax.broadcasted_iota(jnp.int32, (1, tq), 1)
    q_pos = q_pos0 + q_lane
    adm_end = jnp.minimum((q_pos // CHUNK + 1) * CHUNK, n_keys)

    def tile_start(j):
        return pl.multiple_of(j * tk, tk)

    def key_pos(j):
        return j * tk + lax.broadcasted_iota(jnp.int32, (tk, tq), 0)

    wit = wit_ref[...] * idx_scale

    for h2 in range(IDX_HEADS // 2):
        qi2t_scr[h2, :, 0:tq] = qit_ref[(2 * h2) * IDX_DIM:(2 * h2 + 1) * IDX_DIM, :]
        qi2t_scr[h2, :, tq:2 * tq] = qit_ref[(2 * h2 + 1) * IDX_DIM:(2 * h2 + 2) * IDX_DIM, :]

    def score_tile(j, carry):
        ki = ki_ref[pl.ds(tile_start(j), tk), :]
        acc = jnp.zeros((tk, tq), F32)
        for h2 in range(IDX_HEADS // 2):
            lg = jnp.maximum(jnp.dot(ki, qi2t_scr[h2], preferred_element_type=F32), 0.0)
            acc = acc + wit[2 * h2:2 * h2 + 1, :] * lg[:, :tq] + wit[2 * h2 + 1:2 * h2 + 2, :] * lg[:, tq:]
        acc = jnp.where(acc == 0.0, 0.0, acc)
        bits = pltpu.bitcast(acc, jnp.int32)
        key = bits ^ ((bits >> 31) & INT_MAX)
        key_scr[pl.ds(tile_start(j), tk), :] = jnp.where(key_pos(j) < adm_end, key, KEY_NEG_INF)
        return carry

    lax.fori_loop(0, n_tiles, score_tile, 0)

    def count(pred):
        def body(j, cnt):
            kt = key_scr[pl.ds(tile_start(j), tk), :]
            hit = pred(kt, j).astype(jnp.int32)
            return cnt + hit.reshape(tk // 8, 8, tq).sum(axis=0)
        cnt = lax.fori_loop(0, n_tiles, body, jnp.zeros((8, tq), jnp.int32))
        return jnp.sum(cnt, axis=0, keepdims=True)

    def thr_step(b, carry):
        thr, cnt_thr = carry
        cand = thr + lax.shift_left(jnp.int32(1), 31 - b)
        c = count(lambda kt, j: kt >= cand)
        ok = c >= n_sel
        return jnp.where(ok, cand, thr), jnp.where(ok, c, cnt_thr)

    thr, cnt_thr = lax.fori_loop(
        0, 32, thr_step, (jnp.full((1, tq), INT_MIN, jnp.int32), jnp.zeros((1, tq), jnp.int32)))

    tie = (cnt_thr > n_sel) & (q_lane < n_q_valid)
    cut_scr[...] = jnp.full((1, tq), INT_MAX, jnp.int32)

    @pl.when(jnp.max(tie.astype(jnp.int32)) > 0)
    def _():
        need = n_sel - count(lambda kt, j: kt > thr)
        n_bits = int(key_scr.shape[0]).bit_length()

        def pos_step(b, cut):
            cand = cut + lax.shift_left(jnp.int32(1), n_bits - 1 - b)
            c = count(lambda kt, j: (kt == thr) & (key_pos(j) < cand))
            return jnp.where(c < need, cand, cut)

        cut = lax.fori_loop(0, n_bits, pos_step, jnp.zeros((1, tq), jnp.int32))
        cut_scr[...] = jnp.where(tie, cut, INT_MAX)

    cut = cut_scr[...]

    lower = lax.broadcasted_iota(jnp.int32, (PAIR, tq), 0) < HEAD_DIM
    for hp in range(N_PAIRS):
        qt = qt_ref[hp * PAIR:(hp + 1) * PAIR, :]
        q2t_scr[hp, :, 0:tq] = jnp.where(lower, qt, jnp.zeros_like(qt))
        q2t_scr[hp, :, tq:2 * tq] = jnp.where(lower, jnp.zeros_like(qt), qt)
    m_scr[...] = jnp.full(m_scr.shape, M_INIT, F32)
    l_scr[...] = jnp.zeros(l_scr.shape, F32)
    acc_scr[...] = jnp.zeros(acc_scr.shape, F32)
    q_rel2 = (2 * (q_pos - q_pos0)).astype(F32)
    thr_adm = jnp.maximum(thr, KEY_NEG_INF)
    cut_adm = jnp.where(thr > KEY_NEG_INF, cut, -1)

    def attend_tile(j, carry):
        start = tile_start(j)

        def issue_scores(hp):
            kmat = k_ref[pl.ds(start, tk), hp * PAIR:(hp + 1) * PAIR]
            s_scrs[hp % len(s_scrs)][...] = jnp.dot(kmat, q2t_scr[hp], preferred_element_type=F32)

        for hp in range(SCORE_LOOKAHEAD):
            issue_scores(hp)
        kt = key_scr[pl.ds(start, tk), :]
        kp = key_pos(j)
        sel = (kt > thr_adm) | ((kt == thr) & (kp <= cut_adm))
        k_rel = (kp - q_pos0).astype(F32)
        g = jnp.where(sel, jnp.minimum(k_rel, q_rel2 - k_rel), NEG)
        for hp in range(N_PAIRS):
            if hp + SCORE_LOOKAHEAD < N_PAIRS:
                issue_scores(hp + SCORE_LOOKAHEAD)
            s_scr = s_scrs[hp % len(s_scrs)]
            m_prev2, l_prev2 = m_scr[hp], l_scr[hp]
            m_new2 = []
            for par in range(2):
                h = 2 * hp + par
                lanes = slice(par * tq, (par + 1) * tq)
                s = s_scr[:, lanes] + slopes[h] * g
                s_scr[:, lanes] = s
                m_new2.append(jnp.maximum(m_prev2[:, lanes], jnp.max(s, axis=0, keepdims=True)))
            m_new2 = jnp.concatenate(m_new2, axis=1)
            alpha2 = jnp.exp2(m_prev2 - m_new2)
            p2 = jnp.exp2(s_scr[...] - m_new2)
            m_scr[hp] = m_new2
            vt = vt_ref[hp * VT_ROWS:(hp + 1) * VT_ROWS, pl.ds(start, tk)]
            pv = jnp.dot(vt, p2.astype(BF16), preferred_element_type=F32)
            l_scr[hp] = alpha2 * l_prev2 + pv[PAIR:PAIR + 1]
            acc_old = acc_scr[hp]
            acc_scr[hp] = jnp.where(lower, alpha2[:, :tq] * acc_old + pv[:PAIR, :tq],
                                    alpha2[:, tq:] * acc_old + pv[:PAIR, tq:])
        return carry

    lax.fori_loop(0, n_tiles, attend_tile, 0)

    for hp in range(N_PAIRS):
        l2 = l_scr[hp]
        inv = jnp.where(lower, 1.0 / l2[:, :tq], 1.0 / l2[:, tq:])
        o_ref[:, hp * PAIR:(hp + 1) * PAIR] = (acc_scr[hp] * inv).T.astype(o_ref.dtype)


def _dsa(qt, qit, wit, k, vt, ki, *, n_batch, n_q, l_pad, tq, n_q_valid, q_off, n_keys, n_sel, resident):
    tk = next(t for t in (512, 384, 256, 128) if l_pad % t == 0)
    n_qb = n_q // tq
    slopes = tuple(float(np.float32(2.0 ** (-8.0 * (h + 1) / N_HEADS))) * LOG2E for h in range(N_HEADS))
    idx_scale = float(IDX_DIM ** -0.5 * IDX_HEADS ** -0.5)
    mode = dict(pipeline_mode=pl.Buffered(1)) if resident else {}

    def qspec(w):
        return pl.BlockSpec((w, tq), lambda b, i: (0, b * n_qb + i))

    kern = functools.partial(_dsa_kernel, tq=tq, tk=tk, n_q_valid=n_q_valid, q_off=q_off, n_keys=n_keys,
                             n_sel=n_sel, slopes=slopes, idx_scale=idx_scale)
    return pl.pallas_call(
        kern,
        grid=(n_batch, n_qb),
        in_specs=[qspec(D_MODEL), qspec(IDX_HEADS * IDX_DIM), qspec(IDX_HEADS),
                  pl.BlockSpec((l_pad, D_MODEL), lambda b, i: (b, 0), **mode),
                  pl.BlockSpec((N_PAIRS * VT_ROWS, l_pad), lambda b, i: (0, b), **mode),
                  pl.BlockSpec((l_pad, IDX_DIM), lambda b, i: (b, 0), **mode)],
        out_specs=pl.BlockSpec((tq, D_MODEL), lambda b, i: (b * n_qb + i, 0)),
        out_shape=jax.ShapeDtypeStruct((n_batch * n_q, D_MODEL), BF16),
        scratch_shapes=[pltpu.VMEM((l_pad, tq), jnp.int32),
                        pltpu.VMEM((1, tq), jnp.int32),
                        pltpu.VMEM((N_PAIRS, PAIR, 2 * tq), BF16),
                        pltpu.VMEM((IDX_HEADS // 2, IDX_DIM, 2 * tq), BF16),
                        pltpu.VMEM((N_PAIRS, 1, 2 * tq), F32),
                        pltpu.VMEM((N_PAIRS, 1, 2 * tq), F32),
                        pltpu.VMEM((N_PAIRS, PAIR, tq), F32)]
                       + [pltpu.VMEM((tk, 2 * tq), F32)] * (SCORE_LOOKAHEAD + 1),
        compiler_params=_params(("parallel", "arbitrary")),
    )(qt, qit, wit, k, vt, ki)


def kernel(x_prompt, x_sample, cache_a_k, cache_a_v, cache_b_k, cache_b_v, cache_b_kidx,
           w_a_qkv, w_a_o, a_rel_bias, w_b_in, w_b_o, w_ff1, w_ff2, ln1_g, ln1_b, ln2_g, ln2_b):
    bsz, seq, _ = x_prompt.shape
    dbsz, dseq, _ = x_sample.shape
    depth = w_ff1.shape[0]
    past = cache_b_k.shape[2]
    alpha = float((2.0 * depth) ** 0.25)
    n_p = bsz * seq
    n_s = dbsz * dseq
    q_scale = float(HEAD_DIM ** -0.5) * LOG2E

    x = jnp.concatenate([x_prompt.reshape(n_p, D_MODEL), x_sample.reshape(n_s, D_MODEL)], axis=0)

    def split(a, width):
        return a[:n_p].reshape(bsz, seq, width), a[n_p:].reshape(dbsz, dseq, width)

    def heads(a):
        return a.reshape(a.shape[:-1] + (N_HEADS, HEAD_DIM))

    ak_p, av_p, ak_s, av_s = [], [], [], []
    bk_p, bv_p, bi_p, bk_s, bv_s, bi_s = [], [], [], [], [], []
    for i in range(depth):
        j = i // 2
        if i % 2 == 0:
            groups = [(0, D_MODEL, [(0, q_scale, "rows")]),
                      (D_MODEL, D_MODEL, [(1, 1.0, "rows"), (2, 1.0, "rows")]),
                      (2 * D_MODEL, D_MODEL, [(3, 1.0, "rows"), (4, 1.0, "rows")])]
            defs = [(D_MODEL, BF16, "rows"), (D_MODEL, F32, "rows"), (D_MODEL, BF16, "rows"),
                    (D_MODEL, F32, "rows"), (D_MODEL, BF16, "rows")]
            q, k32, k16, v32, v16 = _proj(x, w_a_qkv[j].astype(BF16), groups, defs)
            q_p, q_s = split(q, D_MODEL)
            k_p, k_s = split(k16, D_MODEL)
            v_p, v_s = split(v16, D_MODEL)
            o_p = _band_prompt(q_p, k_p, v_p, a_rel_bias[j]).reshape(n_p, D_MODEL)
            a_len = cache_a_k.shape[2]
            ck = cache_a_k[j].reshape(dbsz, a_len, D_MODEL).astype(BF16)
            cv = cache_a_v[j].reshape(dbsz, a_len, D_MODEL).astype(BF16)
            o_s = _band_sample(q_s, k_s, v_s, ck, cv, a_rel_bias[j], past).reshape(n_s, D_MODEL)
            k32_p, k32_s = split(k32, D_MODEL)
            v32_p, v32_s = split(v32, D_MODEL)
            n_keep = min(N_PREV_CHUNKS * CHUNK, seq)
            ak_p.append(heads(k32_p[:, seq - n_keep:])); av_p.append(heads(v32_p[:, seq - n_keep:]))
            ak_s.append(heads(k32_s)); av_s.append(heads(v32_s))
            w_o = w_a_o[j]
        else:
            w_in = w_b_in[j]
            w_cat = jnp.concatenate(
                [w_in[:, :3 * D_MODEL], jnp.pad(w_in[:, 3 * D_MODEL:], ((0, 0), (0, IDX_PAD - IDX_WIDTH)))],
                axis=1).astype(BF16)
            groups = [(0, D_MODEL, [(0, q_scale, "cols")]),
                      (D_MODEL, D_MODEL, [(1, 1.0, "rows"), (2, 1.0, "rows")]),
                      (2 * D_MODEL, D_MODEL, [(3, 1.0, "rows"), (4, 1.0, "pair_cols")]),
                      (3 * D_MODEL, IDX_PAD, [(5, 1.0, "rows"), (6, 1.0, "cols")])]
            defs = [(D_MODEL, BF16, "cols"), (D_MODEL, F32, "rows"), (D_MODEL, BF16, "rows"),
                    (D_MODEL, F32, "rows"), (N_PAIRS * VT_ROWS, BF16, "pair_cols"),
                    (IDX_PAD, F32, "rows"), (IDX_PAD, F32, "cols")]
            qt, k32, k16, v32, vt, idx, idxt = _proj(x, w_cat, groups, defs)
            n_qi = IDX_HEADS * IDX_DIM
            qit = idxt[:n_qi].astype(BF16)
            wit = idxt[n_qi + IDX_DIM:IDX_WIDTH]
            ki32 = idx[:, n_qi:n_qi + IDX_DIM]
            ki16 = ki32.astype(BF16)

            tq = 2 * CHUNK
            o_p = _dsa(qt, qit, wit, k16, vt, ki16, n_batch=bsz, n_q=seq, l_pad=seq, tq=tq, n_q_valid=tq,
                       q_off=0, n_keys=seq, n_sel=min(TOPK_MAX, seq // 4), resident=True)

            n_keys = past + dseq
            l_pad = -(-n_keys // 128) * 128
            tq_s = -(-dseq // tq) * tq

            def pad_q(a):
                w = a.shape[0]
                a = jnp.pad(a[:, n_p:].reshape(w, dbsz, dseq), ((0, 0), (0, 0), (0, tq_s - dseq)))
                return a.reshape(w, dbsz * tq_s)

            def keys_rows(cache, new):
                w = new.shape[-1]
                a = jnp.concatenate([cache.astype(BF16), new.reshape(dbsz, dseq, w)], axis=1)
                return jnp.pad(a, ((0, 0), (0, l_pad - n_keys), (0, 0))).reshape(dbsz * l_pad, w)

            k_all = keys_rows(cache_b_k[j].reshape(dbsz, past, D_MODEL), k16[n_p:])
            ki_all = keys_rows(cache_b_kidx[j], ki16[n_p:])
            cvt = jnp.transpose(cache_b_v[j].reshape(dbsz, past, N_PAIRS, PAIR).astype(BF16), (2, 3, 0, 1))
            cvt = jnp.concatenate([cvt, jnp.ones((N_PAIRS, ONES_ROWS, dbsz, past), BF16)], axis=1)
            vt_all = jnp.concatenate([cvt.reshape(N_PAIRS * VT_ROWS, dbsz, past),
                                      vt[:, n_p:].reshape(N_PAIRS * VT_ROWS, dbsz, dseq)], axis=2)
            vt_all = jnp.pad(vt_all, ((0, 0), (0, 0), (0, l_pad - n_keys))).reshape(N_PAIRS * VT_ROWS, dbsz * l_pad)
            o_s = _dsa(pad_q(qt), pad_q(qit), pad_q(wit), k_all, vt_all, ki_all, n_batch=dbsz, n_q=tq_s,
                       l_pad=l_pad, tq=tq, n_q_valid=dseq, q_off=past, n_keys=n_keys,
                       n_sel=min(TOPK_MAX, n_keys // 4), resident=False)
            o_s = o_s.reshape(dbsz, tq_s, D_MODEL)[:, :dseq].reshape(n_s, D_MODEL)

            k32_p, k32_s = split(k32, D_MODEL)
            v32_p, v32_s = split(v32, D_MODEL)
            ki_p, ki_s = split(ki32, IDX_DIM)
            bk_p.append(heads(k32_p)); bv_p.append(heads(v32_p)); bi_p.append(ki_p)
            bk_s.append(heads(k32_s)); bv_s.append(heads(v32_s)); bi_s.append(ki_s)
            w_o = w_b_o[j]

        o = jnp.concatenate([o_p, o_s], axis=0)
        x = _oproj_ln(o, w_o.astype(BF16), x, ln1_g[i], ln1_b[i], alpha)
        x = _ffn_ln(x, w_ff1[i].astype(BF16), w_ff2[i].astype(BF16), ln2_g[i], ln2_b[i], alpha)

    y_p, y_s = split(x, D_MODEL)
    return (y_p, y_s,
            jnp.stack(ak_p), jnp.stack(av_p), jnp.stack(bk_p), jnp.stack(bv_p), jnp.stack(bi_p),
            jnp.stack(ak_s), jnp.stack(av_s), jnp.stack(bk_s), jnp.stack(bv_s), jnp.stack(bi_s))
```

```python
import functools
import math

import numpy as np
import jax
import jax.numpy as jnp
from jax import lax
from jax.experimental import pallas as pl
from jax.experimental.pallas import tpu as pltpu

D_MODEL = 1024
N_HEADS = 16
HEAD_DIM = 64
PAIR = 2 * HEAD_DIM
N_PAIRS = N_HEADS // 2
CHUNK = 64
N_PREV_CHUNKS = 8
REL_CLIP = 128
IDX_HEADS = 8
IDX_DIM = 64
IDX_WIDTH = IDX_HEADS * IDX_DIM + IDX_DIM + IDX_HEADS
IDX_PAD = 640
TOPK_MAX = 256
LN_EPS = 1e-5
LOG2E = math.log2(math.e)

NEG = -1e30
M_INIT = -3e38
SCORE_LOOKAHEAD = 3
DSA_PROMPT_QUERIES = 128
DSA_TILE_ELEMS = 512 * 128
ONES_ROWS = 16
VT_ROWS = PAIR + ONES_ROWS
KEY_NEG_INF = -2139095041
INT_MIN = -2147483648
INT_MAX = 2147483647
V7X_VMEM_LIMIT = 60 * 1024 * 1024

F32 = jnp.float32
BF16 = jnp.bfloat16


def _params(sem, flags=None):
    return pltpu.CompilerParams(dimension_semantics=sem, vmem_limit_bytes=V7X_VMEM_LIMIT, flags=flags)


def _row_tile(m, cap):
    t = cap
    while m % t:
        t //= 2
    return t


def _const_spec(shape):
    nd = len(shape)
    return pl.BlockSpec(shape, lambda *_: (0,) * nd, pipeline_mode=pl.Buffered(1))


def _proj_kernel(x_ref, w_ref, *out_refs, groups):
    xb = x_ref[...].astype(BF16)
    for col, width, outs in groups:
        r = jnp.dot(xb, w_ref[:, col:col + width].astype(BF16), preferred_element_type=F32)
        for idx, scale, layout in outs:
            o = out_refs[idx]
            v = r * scale if scale != 1.0 else r
            if layout == "rows":
                o[...] = v.astype(o.dtype)
            elif layout == "cols":
                o[...] = v.T.astype(o.dtype)
            else:
                vt = v.T.astype(o.dtype)
                for hp in range(N_PAIRS):
                    o[hp * VT_ROWS:hp * VT_ROWS + PAIR, :] = vt[hp * PAIR:(hp + 1) * PAIR]
                    o[hp * VT_ROWS + PAIR:(hp + 1) * VT_ROWS, :] = jnp.ones((ONES_ROWS, vt.shape[1]), o.dtype)


def _proj(x, w, groups, out_defs):
    m = x.shape[0]
    tm = _row_tile(m, 512)
    out_shape = [jax.ShapeDtypeStruct((m, wd) if lay == "rows" else (wd, m), dt) for wd, dt, lay in out_defs]
    out_specs = [pl.BlockSpec((tm, wd), lambda i: (i, 0)) if lay == "rows" else pl.BlockSpec((wd, tm), lambda i: (0, i))
                 for wd, _, lay in out_defs]
    return pl.pallas_call(
        functools.partial(_proj_kernel, groups=groups),
        grid=(m // tm,),
        in_specs=[pl.BlockSpec((tm, D_MODEL), lambda i: (i, 0)), _const_spec(w.shape)],
        out_specs=out_specs,
        out_shape=out_shape,
        compiler_params=_params(("parallel",)),
    )(x, w)


def _deepnorm_ln(x, r, g, b, alpha):
    y = alpha * x + r
    mu = jnp.mean(y, axis=-1, keepdims=True)
    yc = y - mu
    var = jnp.mean(yc * yc, axis=-1, keepdims=True)
    return yc * lax.rsqrt(var + LN_EPS) * g + b


def _oproj_ln_kernel(o_ref, w_ref, x_ref, g_ref, b_ref, out_ref, *, alpha):
    r = jnp.dot(o_ref[...], w_ref[...].astype(BF16), preferred_element_type=F32)
    out_ref[...] = _deepnorm_ln(x_ref[...], r, g_ref[...], b_ref[...], alpha)


def _oproj_ln(o, w, x, g, b, alpha):
    m = x.shape[0]
    tm = _row_tile(m, 512)
    row = pl.BlockSpec((tm, D_MODEL), lambda i: (i, 0))
    return pl.pallas_call(
        functools.partial(_oproj_ln_kernel, alpha=alpha),
        grid=(m // tm,),
        in_specs=[row, _const_spec(w.shape), row, _const_spec((1, D_MODEL)), _const_spec((1, D_MODEL))],
        out_specs=row,
        out_shape=jax.ShapeDtypeStruct((m, D_MODEL), F32),
        compiler_params=_params(("parallel",)),
    )(o, w, x, g.reshape(1, D_MODEL), b.reshape(1, D_MODEL))


def _ffn_ln_kernel(x_ref, w1_ref, w2_ref, g_ref, b_ref, out_ref, acc_ref, *, alpha):
    f = pl.program_id(1)

    @pl.when(f == 0)
    def _():
        acc_ref[...] = jnp.zeros_like(acc_ref)

    h = jnp.dot(x_ref[...].astype(BF16), w1_ref[...].astype(BF16), preferred_element_type=F32)
    h = jnp.maximum(h, 0.0)
    acc_ref[...] += jnp.dot((h * h).astype(BF16), w2_ref[...].astype(BF16), preferred_element_type=F32)

    @pl.when(f == pl.num_programs(1) - 1)
    def _():
        out_ref[...] = _deepnorm_ln(x_ref[...], acc_ref[...], g_ref[...], b_ref[...], alpha)


def _ffn_ln(x, w1, w2, g, b, alpha):
    m = x.shape[0]
    d_ff = w1.shape[1]
    tm = _row_tile(m, 1024)
    tf = _row_tile(d_ff, 512)
    row = pl.BlockSpec((tm, D_MODEL), lambda i, f: (i, 0))
    return pl.pallas_call(
        functools.partial(_ffn_ln_kernel, alpha=alpha),
        grid=(m // tm, d_ff // tf),
        in_specs=[row,
                  pl.BlockSpec((D_MODEL, tf), lambda i, f: (0, f)),
                  pl.BlockSpec((tf, D_MODEL), lambda i, f: (f, 0)),
                  _const_spec((1, D_MODEL)), _const_spec((1, D_MODEL))],
        out_specs=row,
        out_shape=jax.ShapeDtypeStruct((m, D_MODEL), F32),
        scratch_shapes=[pltpu.VMEM((tm, D_MODEL), F32)],
        compiler_params=_params(("parallel", "arbitrary")),
    )(x, w1, w2, g.reshape(1, D_MODEL), b.reshape(1, D_MODEL))


def _band_kernel(*refs, seg_counts, n_lead_invalid_fn, tq):
    n_seg = sum(seg_counts)
    q_ref = refs[0]
    k_refs = refs[1:1 + n_seg]
    v_refs = refs[1 + n_seg:1 + 2 * n_seg]
    bias_refs = refs[1 + 2 * n_seg:1 + 2 * n_seg + len(seg_counts)]
    o_ref = refs[-1]

    lane = lax.broadcasted_iota(jnp.int32, (tq, PAIR), 1)
    upper = lane >= HEAD_DIM
    for hp in range(N_PAIRS):
        cols = slice(hp * PAIR, (hp + 1) * PAIR)
        qp = q_ref[0, :, cols]
        kg, vg = [], []
        s0 = 0
        for cnt in seg_counts:
            ks = [k_refs[s0 + t][0, :, cols] for t in range(cnt)]
            vs = [v_refs[s0 + t][0, :, cols] for t in range(cnt)]
            kg.append(ks[0] if cnt == 1 else jnp.concatenate(ks, axis=0))
            vg.append(vs[0] if cnt == 1 else jnp.concatenate(vs, axis=0))
            s0 += cnt
        outs = []
        for par in range(2):
            h = 2 * hp + par
            qm = jnp.where(upper == (par == 1), qp, jnp.zeros_like(qp))
            s = []
            for g, kmat in enumerate(kg):
                sg = lax.dot_general(qm, kmat, (((1,), (1,)), ((), ())), preferred_element_type=F32)
                sg = sg + bias_refs[g][h]
                if g == 0 and n_lead_invalid_fn is not None:
                    col = lax.broadcasted_iota(jnp.int32, sg.shape, 1)
                    sg = jnp.where(col >= n_lead_invalid_fn(pl.program_id(1)), sg, NEG)
                s.append(sg)
            m = functools.reduce(jnp.maximum, [jnp.max(sg, axis=1, keepdims=True) for sg in s])
            p = [jnp.exp2(sg - m) for sg in s]
            l = functools.reduce(jnp.add, [jnp.sum(pg, axis=1, keepdims=True) for pg in p])
            o = functools.reduce(jnp.add, [jnp.dot(pg.astype(BF16), vmat, preferred_element_type=F32)
                                           for pg, vmat in zip(p, vg)])
            outs.append(o / l)
        o_ref[0, :, cols] = jnp.where(upper, outs[1], outs[0]).astype(o_ref.dtype)


def _band_bias(tab, q0, n_q, k0, n_k):
    span = n_q + n_k - 1
    rel = (q0 - k0) + (n_q - 1) - jnp.arange(span)
    u = tab[:, jnp.clip(rel, -REL_CLIP, REL_CLIP) + REL_CLIP].astype(F32) * LOG2E
    n_h = tab.shape[0]
    x = jnp.broadcast_to(u[:, None, :], (n_h, n_q + 1, span)).reshape(n_h, (n_q + 1) * span)
    x = x[:, :n_q * (span + 1)].reshape(n_h, n_q, span + 1)
    bias = x[:, ::-1, :n_k]
    qch = (q0 + jnp.arange(n_q))[:, None] // CHUNK
    kch = (k0 + jnp.arange(n_k))[None, :] // CHUNK
    valid = (kch <= qch) & (kch >= qch - N_PREV_CHUNKS)
    return jnp.where(valid[None], bias, NEG)


def _band_prompt(q, k, v, tab):
    bsz, t, _ = q.shape
    tq = 2 * CHUNK
    n_seg = (N_PREV_CHUNKS * CHUNK) // tq + 1
    lead = (n_seg - 1) * tq
    bias = _band_bias(tab, lead, tq, 0, n_seg * tq)

    def kv_spec(s):
        return pl.BlockSpec((1, tq, D_MODEL), lambda b, i: (b, jnp.maximum(i - (n_seg - 1) + s, 0), 0))

    qspec = pl.BlockSpec((1, tq, D_MODEL), lambda b, i: (b, i, 0))
    kern = functools.partial(_band_kernel, seg_counts=(n_seg,), tq=tq,
                             n_lead_invalid_fn=lambda i: lead - i * tq)
    return pl.pallas_call(
        kern,
        grid=(bsz, t // tq),
        in_specs=[qspec] + [kv_spec(s) for s in range(n_seg)] * 2 + [_const_spec(bias.shape)],
        out_specs=qspec,
        out_shape=jax.ShapeDtypeStruct(q.shape, BF16),
        compiler_params=_params(("parallel", "parallel")),
    )(q, *([k] * n_seg), *([v] * n_seg), bias)


def _band_sample(q, k_new, v_new, cache_k, cache_v, tab, past):
    bsz, s_len, _ = q.shape
    a_len = cache_k.shape[1]
    bias_c = _band_bias(tab, past, s_len, past - a_len, a_len)
    bias_n = _band_bias(tab, past, s_len, past, s_len)

    def spec(n):
        return pl.BlockSpec((1, n, D_MODEL), lambda b, i: (b, 0, 0))

    kern = functools.partial(_band_kernel, seg_counts=(1, 1), tq=s_len, n_lead_invalid_fn=None)
    return pl.pallas_call(
        kern,
        grid=(bsz, 1),
        in_specs=[spec(s_len), spec(a_len), spec(s_len), spec(a_len), spec(s_len),
                  _const_spec(bias_c.shape), _const_spec(bias_n.shape)],
        out_specs=spec(s_len),
        out_shape=jax.ShapeDtypeStruct(q.shape, BF16),
        compiler_params=_params(("parallel", "arbitrary")),
    )(q, cache_k, k_new, cache_v, v_new, bias_c, bias_n)


def _dsa_kernel(qt_ref, qit_ref, wit_ref, k_ref, vt_ref, ki_ref, o_ref,
                key_scr, gmax_scr, cut_scr, q2t_scr, qi2t_scr, m_scr, l_scr, acc_scr, *s_scrs,
                tq, tk, n_q_valid, q_off, n_keys, n_sel, slopes, idx_scale):
    i = pl.program_id(1)
    q_pos0 = q_off + i * tq
    last_chunk = (q_pos0 + tq - 1) // CHUNK
    n_adm = jnp.minimum((last_chunk + 1) * CHUNK, n_keys)
    n_tiles = (n_adm + tk - 1) // tk

    q_lane = lax.broadcasted_iota(jnp.int32, (1, tq), 1)
    q_pos = q_pos0 + q_lane
    adm_end = jnp.minimum((q_pos // CHUNK + 1) * CHUNK, n_keys)

    def tile_start(j):
        return pl.multiple_of(j * tk, tk)

    def key_pos(j):
        return j * tk + lax.broadcasted_iota(jnp.int32, (tk, tq), 0)

    wit = wit_ref[...] * idx_scale

    for h2 in range(IDX_HEADS // 2):
        qi2t_scr[h2, :, 0:tq] = qit_ref[(2 * h2) * IDX_DIM:(2 * h2 + 1) * IDX_DIM, :]
        qi2t_scr[h2, :, tq:2 * tq] = qit_ref[(2 * h2 + 1) * IDX_DIM:(2 * h2 + 2) * IDX_DIM, :]

    def score_tile(j, carry):
        ki = ki_ref[pl.ds(tile_start(j), tk), :]
        acc = jnp.zeros((tk, tq), F32)
        for h2 in range(IDX_HEADS // 2):
            lg = jnp.maximum(jnp.dot(ki, qi2t_scr[h2], preferred_element_type=F32), 0.0)
            acc = acc + wit[2 * h2:2 * h2 + 1, :] * lg[:, :tq] + wit[2 * h2 + 1:2 * h2 + 2, :] * lg[:, tq:]
        acc = jnp.where(acc == 0.0, 0.0, acc)
        bits = pltpu.bitcast(acc, jnp.int32)
        key = bits ^ ((bits >> 31) & INT_MAX)
        key = jnp.where(key_pos(j) < adm_end, key, KEY_NEG_INF)
        key_scr[pl.ds(tile_start(j), tk), :] = key
        gmax_scr[...] = jnp.maximum(gmax_scr[...], key.reshape(tk // TOPK_MAX, TOPK_MAX, tq).max(axis=0))
        return carry

    gmax_scr[...] = jnp.full(gmax_scr.shape, INT_MIN, jnp.int32)
    lax.fori_loop(0, n_tiles, score_tile, 0)

    def count(pred):
        def body(j, cnt):
            kt = key_scr[pl.ds(tile_start(j), tk), :]
            hit = pred(kt, j).astype(jnp.int32)
            return cnt + hit.reshape(tk // 8, 8, tq).sum(axis=0)
        cnt = lax.fori_loop(0, n_tiles, body, jnp.zeros((8, tq), jnp.int32))
        return jnp.sum(cnt, axis=0, keepdims=True)

    gmax = gmax_scr[...]
    lo0 = jnp.min(gmax, axis=0, keepdims=True)
    top = jnp.max(gmax, axis=0, keepdims=True)
    hi0 = jnp.where(top == INT_MAX, top, top + 1)
    n_halvings = jnp.max(32 - lax.clz(hi0 - lo0 - 1))

    def thr_step(_, carry):
        lo, hi = carry
        mid = lo + lax.shift_right_logical(hi - lo, 1)
        ok = count(lambda kt, j: kt >= mid) >= n_sel
        return jnp.where(ok, mid, lo), jnp.where(ok, hi, mid)

    thr, _ = lax.fori_loop(0, n_halvings, thr_step, (lo0, hi0))
    cnt_thr = count(lambda kt, j: kt >= thr)

    tie = (cnt_thr > n_sel) & (q_lane < n_q_valid)
    cut_scr[...] = jnp.full((1, tq), INT_MAX, jnp.int32)

    @pl.when(jnp.max(tie.astype(jnp.int32)) > 0)
    def _():
        need = n_sel - count(lambda kt, j: kt > thr)
        n_bits = int(key_scr.shape[0]).bit_length()

        def pos_step(b, cut):
            cand = cut + lax.shift_left(jnp.int32(1), n_bits - 1 - b)
            c = count(lambda kt, j: (kt == thr) & (key_pos(j) < cand))
            return jnp.where(c < need, cand, cut)

        cut = lax.fori_loop(0, n_bits, pos_step, jnp.zeros((1, tq), jnp.int32))
        cut_scr[...] = jnp.where(tie, cut, INT_MAX)

    cut = cut_scr[...]

    lower = lax.broadcasted_iota(jnp.int32, (PAIR, tq), 0) < HEAD_DIM
    for hp in range(N_PAIRS):
        qt = qt_ref[hp * PAIR:(hp + 1) * PAIR, :]
        q2t_scr[hp, :, 0:tq] = jnp.where(lower, qt, jnp.zeros_like(qt))
        q2t_scr[hp, :, tq:2 * tq] = jnp.where(lower, jnp.zeros_like(qt), qt)
    m_scr[...] = jnp.full(m_scr.shape, M_INIT, F32)
    l_scr[...] = jnp.zeros(l_scr.shape, F32)
    acc_scr[...] = jnp.zeros(acc_scr.shape, F32)
    q_rel2 = (2 * (q_pos - q_pos0)).astype(F32)
    thr_adm = jnp.maximum(thr, KEY_NEG_INF)
    cut_adm = jnp.where(thr > KEY_NEG_INF, cut, -1)

    def attend_tile(j, carry):
        start = tile_start(j)

        def issue_scores(hp):
            kmat = k_ref[pl.ds(start, tk), hp * PAIR:(hp + 1) * PAIR]
            s_scrs[hp % len(s_scrs)][...] = jnp.dot(kmat, q2t_scr[hp], preferred_element_type=F32)

        for hp in range(SCORE_LOOKAHEAD):
            issue_scores(hp)
        kt = key_scr[pl.ds(start, tk), :]
        kp = key_pos(j)
        sel = (kt > thr_adm) | ((kt == thr) & (kp <= cut_adm))
        k_rel = (kp - q_pos0).astype(F32)
        g = jnp.where(sel, jnp.minimum(k_rel, q_rel2 - k_rel), NEG)
        for hp in range(N_PAIRS):
            if hp + SCORE_LOOKAHEAD < N_PAIRS:
                issue_scores(hp + SCORE_LOOKAHEAD)
            s_scr = s_scrs[hp % len(s_scrs)]
            m_prev2, l_prev2 = m_scr[hp], l_scr[hp]
            m_new2 = []
            for par in range(2):
                h = 2 * hp + par
                lanes = slice(par * tq, (par + 1) * tq)
                s = s_scr[:, lanes] + slopes[h] * g
                s_scr[:, lanes] = s
                m_new2.append(jnp.maximum(m_prev2[:, lanes], jnp.max(s, axis=0, keepdims=True)))
            m_new2 = jnp.concatenate(m_new2, axis=1)
            alpha2 = jnp.exp2(m_prev2 - m_new2)
            p2 = jnp.exp2(s_scr[...] - m_new2).astype(BF16)
            m_scr[hp] = m_new2
            vt = vt_ref[hp * VT_ROWS:(hp + 1) * VT_ROWS, pl.ds(start, tk)]
            pv = jnp.dot(vt, p2, preferred_element_type=F32)
            l_scr[hp] = alpha2 * l_prev2 + pv[PAIR:PAIR + 1]
            acc_old = acc_scr[hp]
            acc_scr[hp] = jnp.where(lower, alpha2[:, :tq] * acc_old + pv[:PAIR, :tq],
                                    alpha2[:, tq:] * acc_old + pv[:PAIR, tq:])
        return carry

    lax.fori_loop(0, n_tiles, attend_tile, 0)

    for hp in range(N_PAIRS):
        l2 = l_scr[hp]
        inv = jnp.where(lower, 1.0 / l2[:, :tq], 1.0 / l2[:, tq:])
        o_ref[:, hp * PAIR:(hp + 1) * PAIR] = (acc_scr[hp] * inv).T.astype(o_ref.dtype)


def _dsa(qt, qit, wit, k, vt, ki, *, n_batch, n_q, l_pad, tq, n_q_valid, q_off, n_keys, n_sel, resident):
    tk = next(t for t in (512, 256) if l_pad % t == 0 and t * tq <= DSA_TILE_ELEMS)
    assert tk % TOPK_MAX == 0 and n_sel <= TOPK_MAX
    n_qb = n_q // tq
    slopes = tuple(float(np.float32(2.0 ** (-8.0 * (h + 1) / N_HEADS))) * LOG2E for h in range(N_HEADS))
    idx_scale = float(IDX_DIM ** -0.5 * IDX_HEADS ** -0.5)
    mode = dict(pipeline_mode=pl.Buffered(1)) if resident else {}

    def qspec(w):
        return pl.BlockSpec((w, tq), lambda b, i: (0, b * n_qb + i))

    kern = functools.partial(_dsa_kernel, tq=tq, tk=tk, n_q_valid=n_q_valid, q_off=q_off, n_keys=n_keys,
                             n_sel=n_sel, slopes=slopes, idx_scale=idx_scale)
    return pl.pallas_call(
        kern,
        grid=(n_batch, n_qb),
        in_specs=[qspec(D_MODEL), qspec(IDX_HEADS * IDX_DIM), qspec(IDX_HEADS),
                  pl.BlockSpec((l_pad, D_MODEL), lambda b, i: (b, 0), **mode),
                  pl.BlockSpec((N_PAIRS * VT_ROWS, l_pad), lambda b, i: (0, b), **mode),
                  pl.BlockSpec((l_pad, IDX_DIM), lambda b, i: (b, 0), **mode)],
        out_specs=pl.BlockSpec((tq, D_MODEL), lambda b, i: (b * n_qb + i, 0)),
        out_shape=jax.ShapeDtypeStruct((n_batch * n_q, D_MODEL), BF16),
        scratch_shapes=[pltpu.VMEM((l_pad, tq), jnp.int32),
                        pltpu.VMEM((TOPK_MAX, tq), jnp.int32),
                        pltpu.VMEM((1, tq), jnp.int32),
                        pltpu.VMEM((N_PAIRS, PAIR, 2 * tq), BF16),
                        pltpu.VMEM((IDX_HEADS // 2, IDX_DIM, 2 * tq), BF16),
                        pltpu.VMEM((N_PAIRS, 1, 2 * tq), F32),
                        pltpu.VMEM((N_PAIRS, 1, 2 * tq), F32),
                        pltpu.VMEM((N_PAIRS, PAIR, tq), F32)]
                       + [pltpu.VMEM((tk, 2 * tq), F32)] * (SCORE_LOOKAHEAD + 1),
        compiler_params=_params(("parallel", "arbitrary")),
    )(qt, qit, wit, k, vt, ki)


def kernel(x_prompt, x_sample, cache_a_k, cache_a_v, cache_b_k, cache_b_v, cache_b_kidx,
           w_a_qkv, w_a_o, a_rel_bias, w_b_in, w_b_o, w_ff1, w_ff2, ln1_g, ln1_b, ln2_g, ln2_b):
    bsz, seq, _ = x_prompt.shape
    dbsz, dseq, _ = x_sample.shape
    depth = w_ff1.shape[0]
    past = cache_b_k.shape[2]
    alpha = float((2.0 * depth) ** 0.25)
    n_p = bsz * seq
    n_s = dbsz * dseq
    q_scale = float(HEAD_DIM ** -0.5) * LOG2E

    x = jnp.concatenate([x_prompt.reshape(n_p, D_MODEL), x_sample.reshape(n_s, D_MODEL)], axis=0)

    def split(a, width):
        return a[:n_p].reshape(bsz, seq, width), a[n_p:].reshape(dbsz, dseq, width)

    def heads(a):
        return a.reshape(a.shape[:-1] + (N_HEADS, HEAD_DIM))

    ak_p, av_p, ak_s, av_s = [], [], [], []
    bk_p, bv_p, bi_p, bk_s, bv_s, bi_s = [], [], [], [], [], []
    for i in range(depth):
        j = i // 2
        if i % 2 == 0:
            groups = [(0, D_MODEL, [(0, q_scale, "rows")]),
                      (D_MODEL, D_MODEL, [(1, 1.0, "rows"), (2, 1.0, "rows")]),
                      (2 * D_MODEL, D_MODEL, [(3, 1.0, "rows"), (4, 1.0, "rows")])]
            defs = [(D_MODEL, BF16, "rows"), (D_MODEL, F32, "rows"), (D_MODEL, BF16, "rows"),
                    (D_MODEL, F32, "rows"), (D_MODEL, BF16, "rows")]
            q, k32, k16, v32, v16 = _proj(x, w_a_qkv[j], groups, defs)
            q_p, q_s = split(q, D_MODEL)
            k_p, k_s = split(k16, D_MODEL)
            v_p, v_s = split(v16, D_MODEL)
            o_p = _band_prompt(q_p, k_p, v_p, a_rel_bias[j]).reshape(n_p, D_MODEL)
            a_len = cache_a_k.shape[2]
            ck = cache_a_k[j].reshape(dbsz, a_len, D_MODEL).astype(BF16)
            cv = cache_a_v[j].reshape(dbsz, a_len, D_MODEL).astype(BF16)
            o_s = _band_sample(q_s, k_s, v_s, ck, cv, a_rel_bias[j], past).reshape(n_s, D_MODEL)
            k32_p, k32_s = split(k32, D_MODEL)
            v32_p, v32_s = split(v32, D_MODEL)
            n_keep = min(N_PREV_CHUNKS * CHUNK, seq)
            ak_p.append(heads(k32_p[:, seq - n_keep:])); av_p.append(heads(v32_p[:, seq - n_keep:]))
            ak_s.append(heads(k32_s)); av_s.append(heads(v32_s))
            w_o = w_a_o[j]
        else:
            w_in = w_b_in[j]
            w_cat = jnp.concatenate(
                [w_in[:, :3 * D_MODEL], jnp.pad(w_in[:, 3 * D_MODEL:], ((0, 0), (0, IDX_PAD - IDX_WIDTH)))],
                axis=1)
            groups = [(0, D_MODEL, [(0, q_scale, "cols")]),
                      (D_MODEL, D_MODEL, [(1, 1.0, "rows"), (2, 1.0, "rows")]),
                      (2 * D_MODEL, D_MODEL, [(3, 1.0, "rows"), (4, 1.0, "pair_cols")]),
                      (3 * D_MODEL, IDX_PAD, [(5, 1.0, "rows"), (6, 1.0, "cols")])]
            defs = [(D_MODEL, BF16, "cols"), (D_MODEL, F32, "rows"), (D_MODEL, BF16, "rows"),
                    (D_MODEL, F32, "rows"), (N_PAIRS * VT_ROWS, BF16, "pair_cols"),
                    (IDX_PAD, F32, "rows"), (IDX_PAD, F32, "cols")]
            qt, k32, k16, v32, vt, idx, idxt = _proj(x, w_cat, groups, defs)
            n_qi = IDX_HEADS * IDX_DIM
            qit = idxt[:n_qi].astype(BF16)
            wit = idxt[n_qi + IDX_DIM:IDX_WIDTH]
            ki32 = idx[:, n_qi:n_qi + IDX_DIM]
            ki16 = ki32.astype(BF16)

            tq_p = _row_tile(seq, DSA_PROMPT_QUERIES)
            o_p = _dsa(qt, qit, wit, k16, vt, ki16, n_batch=bsz, n_q=seq, l_pad=seq, tq=tq_p, n_q_valid=tq_p,
                       q_off=0, n_keys=seq, n_sel=min(TOPK_MAX, seq // 4), resident=True)
            tq = 2 * CHUNK

            n_keys = past + dseq
            l_pad = -(-n_keys // TOPK_MAX) * TOPK_MAX
            tq_s = -(-dseq // tq) * tq

            def pad_q(a):
                w = a.shape[0]
                a = jnp.pad(a[:, n_p:].reshape(w, dbsz, dseq), ((0, 0), (0, 0), (0, tq_s - dseq)))
                return a.reshape(w, dbsz * tq_s)

            def keys_rows(cache, new):
                w = new.shape[-1]
                a = jnp.concatenate([cache.astype(BF16), new.reshape(dbsz, dseq, w)], axis=1)
                return jnp.pad(a, ((0, 0), (0, l_pad - n_keys), (0, 0))).reshape(dbsz * l_pad, w)

            k_all = keys_rows(cache_b_k[j].reshape(dbsz, past, D_MODEL), k16[n_p:])
            ki_all = keys_rows(cache_b_kidx[j], ki16[n_p:])
            cvt = jnp.transpose(cache_b_v[j].reshape(dbsz, past, N_PAIRS, PAIR).astype(BF16), (2, 3, 0, 1))
            cvt = jnp.concatenate([cvt, jnp.ones((N_PAIRS, ONES_ROWS, dbsz, past), BF16)], axis=1)
            vt_all = jnp.concatenate([cvt.reshape(N_PAIRS * VT_ROWS, dbsz, past),
                                      vt[:, n_p:].reshape(N_PAIRS * VT_ROWS, dbsz, dseq)], axis=2)
            vt_all = jnp.pad(vt_all, ((0, 0), (0, 0), (0, l_pad - n_keys))).reshape(N_PAIRS * VT_ROWS, dbsz * l_pad)
            o_s = _dsa(pad_q(qt), pad_q(qit), pad_q(wit), k_all, vt_all, ki_all, n_batch=dbsz, n_q=tq_s,
                       l_pad=l_pad, tq=tq, n_q_valid=dseq, q_off=past, n_keys=n_keys,
                       n_sel=min(TOPK_MAX, n_keys // 4), resident=False)
            o_s = o_s.reshape(dbsz, tq_s, D_MODEL)[:, :dseq].reshape(n_s, D_MODEL)

            k32_p, k32_s = split(k32, D_MODEL)
            v32_p, v32_s = split(v32, D_MODEL)
            ki_p, ki_s = split(ki32, IDX_DIM)
            bk_p.append(heads(k32_p)); bv_p.append(heads(v32_p)); bi_p.append(ki_p)
            bk_s.append(heads(k32_s)); bv_s.append(heads(v32_s)); bi_s.append(ki_s)
            w_o = w_b_o[j]

        o = jnp.concatenate([o_p, o_s], axis=0)
        x = _oproj_ln(o, w_o, x, ln1_g[i], ln1_b[i], alpha)
        x = _ffn_ln(x, w_ff1[i], w_ff2[i], ln2_g[i], ln2_b[i], alpha)

    y_p, y_s = split(x, D_MODEL)
    return (y_p, y_s,
            jnp.stack(ak_p), jnp.stack(av_p), jnp.stack(bk_p), jnp.stack(bv_p), jnp.stack(bi_p),
            jnp.stack(ak_s), jnp.stack(av_s), jnp.stack(bk_s), jnp.stack(bv_s), jnp.stack(bi_s))
```

```python
import functools
import math

import numpy as np
import jax
import jax.numpy as jnp
from jax import lax
from jax.experimental import pallas as pl
from jax.experimental.pallas import tpu as pltpu

D_MODEL = 1024
N_HEADS = 16
HEAD_DIM = 64
PAIR = 2 * HEAD_DIM
N_PAIRS = N_HEADS // 2
CHUNK = 64
N_PREV_CHUNKS = 8
REL_CLIP = 128
IDX_HEADS = 8
IDX_DIM = 64
IDX_WIDTH = IDX_HEADS * IDX_DIM + IDX_DIM + IDX_HEADS
IDX_PAD = 640
TOPK_MAX = 256
LN_EPS = 1e-5
LOG2E = math.log2(math.e)

NEG = -1e30
M_INIT = -3e38
SCORE_LOOKAHEAD = 3
DSA_PROMPT_QUERIES = 128
DSA_TILE_ELEMS = 512 * 128
ONES_ROWS = 16
VT_ROWS = PAIR + ONES_ROWS
KEY_NEG_INF = -2139095041
INT_MIN = -2147483648
INT_MAX = 2147483647
V7X_VMEM_LIMIT = 60 * 1024 * 1024

F32 = jnp.float32
BF16 = jnp.bfloat16


def _params(sem, flags=None):
    return pltpu.CompilerParams(dimension_semantics=sem, vmem_limit_bytes=V7X_VMEM_LIMIT, flags=flags)


def _row_tile(m, cap):
    t = cap
    while m % t:
        t //= 2
    return t


def _const_spec(shape):
    nd = len(shape)
    return pl.BlockSpec(shape, lambda *_: (0,) * nd, pipeline_mode=pl.Buffered(1))


def _proj_kernel(x_ref, w_ref, *out_refs, groups, n_head_blocks):
    i = pl.program_id(0)
    xb = x_ref[...].astype(BF16)
    for col, width, outs in groups:
        r = jnp.dot(xb, w_ref[:, col:col + width].astype(BF16), preferred_element_type=F32)
        for idx, scale, layout in outs:
            o = out_refs[idx]
            v = r * scale if scale != 1.0 else r
            if layout == "rows":
                o[...] = v.astype(o.dtype)
            elif layout in ("rows_head", "rows_tail"):
                @pl.when((i < n_head_blocks) if layout == "rows_head" else (i >= n_head_blocks))
                def _(o=o, v=v):
                    o[...] = v.astype(o.dtype)
            elif layout == "cols":
                o[...] = v.T.astype(o.dtype)
            else:
                vt = v.T.astype(o.dtype)
                for hp in range(N_PAIRS):
                    o[hp * VT_ROWS:hp * VT_ROWS + PAIR, :] = vt[hp * PAIR:(hp + 1) * PAIR]
                    o[hp * VT_ROWS + PAIR:(hp + 1) * VT_ROWS, :] = jnp.ones((ONES_ROWS, vt.shape[1]), o.dtype)


def _proj(x, w, groups, out_defs, n_head):
    m = x.shape[0]
    tm = _row_tile(math.gcd(m, n_head), 512)
    nhb = n_head // tm
    shapes = {"rows": lambda wd: (m, wd), "rows_head": lambda wd: (n_head, wd), "rows_tail": lambda wd: (m - n_head, wd),
              "cols": lambda wd: (wd, m), "pair_cols": lambda wd: (wd, m)}
    specs = {"rows": lambda wd: pl.BlockSpec((tm, wd), lambda i: (i, 0)),
             "rows_head": lambda wd: pl.BlockSpec((tm, wd), lambda i: (jnp.minimum(i, nhb - 1), 0)),
             "rows_tail": lambda wd: pl.BlockSpec((tm, wd), lambda i: (jnp.maximum(i - nhb, 0), 0)),
             "cols": lambda wd: pl.BlockSpec((wd, tm), lambda i: (0, i)),
             "pair_cols": lambda wd: pl.BlockSpec((wd, tm), lambda i: (0, i))}
    out_shape = [jax.ShapeDtypeStruct(shapes[lay](wd), dt) for wd, dt, lay in out_defs]
    out_specs = [specs[lay](wd) for wd, _, lay in out_defs]
    return pl.pallas_call(
        functools.partial(_proj_kernel, groups=groups, n_head_blocks=nhb),
        grid=(m // tm,),
        in_specs=[pl.BlockSpec((tm, D_MODEL), lambda i: (i, 0)), _const_spec(w.shape)],
        out_specs=out_specs,
        out_shape=out_shape,
        compiler_params=_params(("arbitrary",)),
    )(x, w)


def _deepnorm_ln(x, r, g, b, alpha):
    y = alpha * x + r
    mu = jnp.mean(y, axis=-1, keepdims=True)
    yc = y - mu
    var = jnp.mean(yc * yc, axis=-1, keepdims=True)
    return yc * lax.rsqrt(var + LN_EPS) * g + b


def _oproj_ln_kernel(o_ref, w_ref, x_ref, g_ref, b_ref, out_ref, *, alpha):
    r = jnp.dot(o_ref[...], w_ref[...].astype(BF16), preferred_element_type=F32)
    out_ref[...] = _deepnorm_ln(x_ref[...], r, g_ref[...], b_ref[...], alpha)


def _oproj_ln(o, w, x, g, b, alpha):
    m = x.shape[0]
    tm = _row_tile(m, 512)
    row = pl.BlockSpec((tm, D_MODEL), lambda i: (i, 0))
    return pl.pallas_call(
        functools.partial(_oproj_ln_kernel, alpha=alpha),
        grid=(m // tm,),
        in_specs=[row, _const_spec(w.shape), row, _const_spec((1, D_MODEL)), _const_spec((1, D_MODEL))],
        out_specs=row,
        out_shape=jax.ShapeDtypeStruct((m, D_MODEL), F32),
        compiler_params=_params(("parallel",)),
    )(o, w, x, g.reshape(1, D_MODEL), b.reshape(1, D_MODEL))


def _ffn_ln_kernel(x_ref, w1_ref, w2_ref, g_ref, b_ref, out_ref, acc_ref, *, alpha):
    f = pl.program_id(1)

    @pl.when(f == 0)
    def _():
        acc_ref[...] = jnp.zeros_like(acc_ref)

    h = jnp.dot(x_ref[...].astype(BF16), w1_ref[...].astype(BF16), preferred_element_type=F32)
    h = jnp.maximum(h, 0.0)
    acc_ref[...] += jnp.dot((h * h).astype(BF16), w2_ref[...].astype(BF16), preferred_element_type=F32)

    @pl.when(f == pl.num_programs(1) - 1)
    def _():
        out_ref[...] = _deepnorm_ln(x_ref[...], acc_ref[...], g_ref[...], b_ref[...], alpha)


def _ffn_ln(x, w1, w2, g, b, alpha):
    m = x.shape[0]
    d_ff = w1.shape[1]
    tm = _row_tile(m, 1024)
    tf = _row_tile(d_ff, 512)
    row = pl.BlockSpec((tm, D_MODEL), lambda i, f: (i, 0))
    return pl.pallas_call(
        functools.partial(_ffn_ln_kernel, alpha=alpha),
        grid=(m // tm, d_ff // tf),
        in_specs=[row,
                  pl.BlockSpec((D_MODEL, tf), lambda i, f: (0, f)),
                  pl.BlockSpec((tf, D_MODEL), lambda i, f: (f, 0)),
                  _const_spec((1, D_MODEL)), _const_spec((1, D_MODEL))],
        out_specs=row,
        out_shape=jax.ShapeDtypeStruct((m, D_MODEL), F32),
        scratch_shapes=[pltpu.VMEM((tm, D_MODEL), F32)],
        compiler_params=_params(("parallel", "arbitrary")),
    )(x, w1, w2, g.reshape(1, D_MODEL), b.reshape(1, D_MODEL))


def _band_kernel(*refs, seg_counts, n_lead_invalid_fn, tq):
    n_seg = sum(seg_counts)
    q_ref = refs[0]
    k_refs = refs[1:1 + n_seg]
    v_refs = refs[1 + n_seg:1 + 2 * n_seg]
    bias_refs = refs[1 + 2 * n_seg:1 + 2 * n_seg + len(seg_counts)]
    o_ref = refs[-1]

    lane = lax.broadcasted_iota(jnp.int32, (tq, PAIR), 1)
    upper = lane >= HEAD_DIM
    for hp in range(N_PAIRS):
        cols = slice(hp * PAIR, (hp + 1) * PAIR)
        qp = q_ref[0, :, cols]
        kg, vg = [], []
        s0 = 0
        for cnt in seg_counts:
            ks = [k_refs[s0 + t][0, :, cols] for t in range(cnt)]
            vs = [v_refs[s0 + t][0, :, cols] for t in range(cnt)]
            kg.append(ks[0] if cnt == 1 else jnp.concatenate(ks, axis=0))
            vg.append(vs[0] if cnt == 1 else jnp.concatenate(vs, axis=0))
            s0 += cnt
        outs = []
        for par in range(2):
            h = 2 * hp + par
            qm = jnp.where(upper == (par == 1), qp, jnp.zeros_like(qp))
            s = []
            for g, kmat in enumerate(kg):
                sg = lax.dot_general(qm, kmat, (((1,), (1,)), ((), ())), preferred_element_type=F32)
                sg = sg + bias_refs[g][h]
                if g == 0 and n_lead_invalid_fn is not None:
                    col = lax.broadcasted_iota(jnp.int32, sg.shape, 1)
                    sg = jnp.where(col >= n_lead_invalid_fn(pl.program_id(1)), sg, NEG)
                s.append(sg)
            m = functools.reduce(jnp.maximum, [jnp.max(sg, axis=1, keepdims=True) for sg in s])
            p = [jnp.exp2(sg - m) for sg in s]
            l = functools.reduce(jnp.add, [jnp.sum(pg, axis=1, keepdims=True) for pg in p])
            o = functools.reduce(jnp.add, [jnp.dot(pg.astype(BF16), vmat, preferred_element_type=F32)
                                           for pg, vmat in zip(p, vg)])
            outs.append(o / l)
        o_ref[0, :, cols] = jnp.where(upper, outs[1], outs[0]).astype(o_ref.dtype)


def _band_bias(tab, q0, n_q, k0, n_k):
    span = n_q + n_k - 1
    rel = (q0 - k0) + (n_q - 1) - jnp.arange(span)
    u = tab[:, jnp.clip(rel, -REL_CLIP, REL_CLIP) + REL_CLIP].astype(F32) * LOG2E
    n_h = tab.shape[0]
    x = jnp.broadcast_to(u[:, None, :], (n_h, n_q + 1, span)).reshape(n_h, (n_q + 1) * span)
    x = x[:, :n_q * (span + 1)].reshape(n_h, n_q, span + 1)
    bias = x[:, ::-1, :n_k]
    qch = (q0 + jnp.arange(n_q))[:, None] // CHUNK
    kch = (k0 + jnp.arange(n_k))[None, :] // CHUNK
    valid = (kch <= qch) & (kch >= qch - N_PREV_CHUNKS)
    return jnp.where(valid[None], bias, NEG)


def _band_prompt_kernel(qt_ref, *refs, n_seg, tq):
    k_refs = refs[:n_seg]
    vt_refs = refs[n_seg:2 * n_seg]
    bias_ref, o_ref = refs[2 * n_seg], refs[2 * n_seg + 1]
    s_scrs = refs[2 * n_seg + 2:]
    n_keys = n_seg * tq
    n_invalid = ((n_seg - 1) - pl.program_id(1)) * tq
    start_mask = jnp.where(lax.broadcasted_iota(jnp.int32, (n_keys, tq), 0) >= n_invalid, 0.0, NEG)
    lower = lax.broadcasted_iota(jnp.int32, (PAIR, tq), 0) < HEAD_DIM

    def issue_scores(hp):
        cols = slice(hp * PAIR, (hp + 1) * PAIR)
        kmat = jnp.concatenate([r[:, cols] for r in k_refs], axis=0)
        qt = qt_ref[cols, :]
        q2t = jnp.concatenate([jnp.where(lower, qt, jnp.zeros_like(qt)),
                               jnp.where(lower, jnp.zeros_like(qt), qt)], axis=1)
        s_scrs[hp % len(s_scrs)][...] = jnp.dot(kmat, q2t, preferred_element_type=F32)

    for hp in range(SCORE_LOOKAHEAD):
        issue_scores(hp)
    for hp in range(N_PAIRS):
        if hp + SCORE_LOOKAHEAD < N_PAIRS:
            issue_scores(hp + SCORE_LOOKAHEAD)
        s_scr = s_scrs[hp % len(s_scrs)]
        ps = []
        for par in range(2):
            s = s_scr[:, par * tq:(par + 1) * tq] + bias_ref[2 * hp + par] + start_mask
            ps.append(jnp.exp2(s - jnp.max(s, axis=0, keepdims=True)).astype(BF16))
        vt = jnp.concatenate([r[hp * VT_ROWS:(hp + 1) * VT_ROWS, :] for r in vt_refs], axis=1)
        pv = jnp.dot(vt, jnp.concatenate(ps, axis=1), preferred_element_type=F32)
        o_t = jnp.where(lower, pv[:PAIR, :tq] / pv[PAIR:PAIR + 1, :tq], pv[:PAIR, tq:] / pv[PAIR:PAIR + 1, tq:])
        o_ref[:, hp * PAIR:(hp + 1) * PAIR] = o_t.T.astype(o_ref.dtype)


def _band_prompt(qt, k, vt, tab, *, n_batch, seq):
    tq = 2 * CHUNK
    n_seg = (N_PREV_CHUNKS * CHUNK) // tq + 1
    n_qb = seq // tq
    bias_t = jnp.swapaxes(_band_bias(tab, (n_seg - 1) * tq, tq, 0, n_seg * tq), 1, 2)

    def seg_block(b, i, s):
        return b * n_qb + jnp.maximum(i - (n_seg - 1) + s, 0)

    k_specs = [pl.BlockSpec((tq, D_MODEL), lambda b, i, s=s: (seg_block(b, i, s), 0)) for s in range(n_seg)]
    vt_specs = [pl.BlockSpec((N_PAIRS * VT_ROWS, tq), lambda b, i, s=s: (0, seg_block(b, i, s)))
                for s in range(n_seg)]
    return pl.pallas_call(
        functools.partial(_band_prompt_kernel, n_seg=n_seg, tq=tq),
        grid=(n_batch, n_qb),
        in_specs=[pl.BlockSpec((D_MODEL, tq), lambda b, i: (0, b * n_qb + i))] + k_specs + vt_specs
                 + [_const_spec(bias_t.shape)],
        out_specs=pl.BlockSpec((tq, D_MODEL), lambda b, i: (b * n_qb + i, 0)),
        out_shape=jax.ShapeDtypeStruct((n_batch * seq, D_MODEL), BF16),
        scratch_shapes=[pltpu.VMEM((n_seg * tq, 2 * tq), F32)] * (SCORE_LOOKAHEAD + 1),
        compiler_params=_params(("parallel", "parallel")),
    )(qt, *([k] * n_seg), *([vt] * n_seg), bias_t)


def _band_sample(q, k_new, v_new, cache_k, cache_v, tab, past):
    bsz, s_len, _ = q.shape
    a_len = cache_k.shape[1]
    bias_c = _band_bias(tab, past, s_len, past - a_len, a_len)
    bias_n = _band_bias(tab, past, s_len, past, s_len)

    def spec(n):
        return pl.BlockSpec((1, n, D_MODEL), lambda b, i: (b, 0, 0))

    kern = functools.partial(_band_kernel, seg_counts=(1, 1), tq=s_len, n_lead_invalid_fn=None)
    return pl.pallas_call(
        kern,
        grid=(bsz, 1),
        in_specs=[spec(s_len), spec(a_len), spec(s_len), spec(a_len), spec(s_len),
                  _const_spec(bias_c.shape), _const_spec(bias_n.shape)],
        out_specs=spec(s_len),
        out_shape=jax.ShapeDtypeStruct(q.shape, BF16),
        compiler_params=_params(("parallel", "arbitrary")),
    )(q, cache_k, k_new, cache_v, v_new, bias_c, bias_n)


def _dsa_kernel(qt_ref, qit_ref, wit_ref, k_ref, vt_ref, ki_ref, o_ref,
                key_scr, gmax_scr, cut_scr, q2t_scr, qi2t_scr, m_scr, l_scr, acc_scr, *s_scrs,
                tq, tk, n_q_valid, q_off, n_keys, n_sel, slopes, idx_scale):
    i = pl.program_id(1)
    q_pos0 = q_off + i * tq
    last_chunk = (q_pos0 + tq - 1) // CHUNK
    n_adm = jnp.minimum((last_chunk + 1) * CHUNK, n_keys)
    n_tiles = (n_adm + tk - 1) // tk

    q_lane = lax.broadcasted_iota(jnp.int32, (1, tq), 1)
    q_pos = q_pos0 + q_lane
    adm_end = jnp.minimum((q_pos // CHUNK + 1) * CHUNK, n_keys)

    def tile_start(j):
        return pl.multiple_of(j * tk, tk)

    def key_pos(j):
        return j * tk + lax.broadcasted_iota(jnp.int32, (tk, tq), 0)

    wit = wit_ref[...] * idx_scale

    for h2 in range(IDX_HEADS // 2):
        qi2t_scr[h2, :, 0:tq] = qit_ref[(2 * h2) * IDX_DIM:(2 * h2 + 1) * IDX_DIM, :]
        qi2t_scr[h2, :, tq:2 * tq] = qit_ref[(2 * h2 + 1) * IDX_DIM:(2 * h2 + 2) * IDX_DIM, :]

    def score_tile(j, carry):
        ki = ki_ref[pl.ds(tile_start(j), tk), :]
        acc = jnp.zeros((tk, tq), F32)
        for h2 in range(IDX_HEADS // 2):
            lg = jnp.maximum(jnp.dot(ki, qi2t_scr[h2], preferred_element_type=F32), 0.0)
            acc = acc + wit[2 * h2:2 * h2 + 1, :] * lg[:, :tq] + wit[2 * h2 + 1:2 * h2 + 2, :] * lg[:, tq:]
        acc = jnp.where(acc == 0.0, 0.0, acc)
        bits = pltpu.bitcast(acc, jnp.int32)
        key = bits ^ ((bits >> 31) & INT_MAX)
        key = jnp.where(key_pos(j) < adm_end, key, KEY_NEG_INF)
        key_scr[pl.ds(tile_start(j), tk), :] = key
        gmax_scr[...] = jnp.maximum(gmax_scr[...], key.reshape(tk // TOPK_MAX, TOPK_MAX, tq).max(axis=0))
        return carry

    gmax_scr[...] = jnp.full(gmax_scr.shape, INT_MIN, jnp.int32)
    lax.fori_loop(0, n_tiles, score_tile, 0)

    def count(pred):
        def body(j, cnt):
            kt = key_scr[pl.ds(tile_start(j), tk), :]
            hit = pred(kt, j).astype(jnp.int32)
            return cnt + hit.reshape(tk // 8, 8, tq).sum(axis=0)
        cnt = lax.fori_loop(0, n_tiles, body, jnp.zeros((8, tq), jnp.int32))
        return jnp.sum(cnt, axis=0, keepdims=True)

    gmax = gmax_scr[...]
    lo0 = jnp.min(gmax, axis=0, keepdims=True)
    top = jnp.max(gmax, axis=0, keepdims=True)
    hi0 = jnp.where(top == INT_MAX, top, top + 1)
    n_halvings = jnp.max(32 - lax.clz(hi0 - lo0 - 1))

    def thr_step(_, carry):
        lo, hi = carry
        mid = lo + lax.shift_right_logical(hi - lo, 1)
        ok = count(lambda kt, j: kt >= mid) >= n_sel
        return jnp.where(ok, mid, lo), jnp.where(ok, hi, mid)

    thr, _ = lax.fori_loop(0, n_halvings, thr_step, (lo0, hi0))
    cnt_thr = count(lambda kt, j: kt >= thr)

    tie = (cnt_thr > n_sel) & (q_lane < n_q_valid)
    cut_scr[...] = jnp.full((1, tq), INT_MAX, jnp.int32)

    @pl.when(jnp.max(tie.astype(jnp.int32)) > 0)
    def _():
        need = n_sel - count(lambda kt, j: kt > thr)
        n_bits = int(key_scr.shape[0]).bit_length()

        def pos_step(b, cut):
            cand = cut + lax.shift_left(jnp.int32(1), n_bits - 1 - b)
            c = count(lambda kt, j: (kt == thr) & (key_pos(j) < cand))
            return jnp.where(c < need, cand, cut)

        cut = lax.fori_loop(0, n_bits, pos_step, jnp.zeros((1, tq), jnp.int32))
        cut_scr[...] = jnp.where(tie, cut, INT_MAX)

    cut = cut_scr[...]

    lower = lax.broadcasted_iota(jnp.int32, (PAIR, tq), 0) < HEAD_DIM
    for hp in range(N_PAIRS):
        qt = qt_ref[hp * PAIR:(hp + 1) * PAIR, :]
        q2t_scr[hp, :, 0:tq] = jnp.where(lower, qt, jnp.zeros_like(qt))
        q2t_scr[hp, :, tq:2 * tq] = jnp.where(lower, jnp.zeros_like(qt), qt)
    m_scr[...] = jnp.full(m_scr.shape, M_INIT, F32)
    l_scr[...] = jnp.zeros(l_scr.shape, F32)
    acc_scr[...] = jnp.zeros(acc_scr.shape, F32)
    q_rel2 = (2 * (q_pos - q_pos0)).astype(F32)
    thr_adm = jnp.maximum(thr, KEY_NEG_INF)
    cut_adm = jnp.where(thr > KEY_NEG_INF, cut, -1)

    def attend_tile(j, carry):
        start = tile_start(j)

        def issue_scores(hp):
            kmat = k_ref[pl.ds(start, tk), hp * PAIR:(hp + 1) * PAIR]
            s_scrs[hp % len(s_scrs)][...] = jnp.dot(kmat, q2t_scr[hp], preferred_element_type=F32)

        for hp in range(SCORE_LOOKAHEAD):
            issue_scores(hp)
        kt = key_scr[pl.ds(start, tk), :]
        kp = key_pos(j)
        sel = (kt > thr_adm) | ((kt == thr) & (kp <= cut_adm))
        k_rel = (kp - q_pos0).astype(F32)
        g = jnp.where(sel, jnp.minimum(k_rel, q_rel2 - k_rel), NEG)
        for hp in range(N_PAIRS):
            if hp + SCORE_LOOKAHEAD < N_PAIRS:
                issue_scores(hp + SCORE_LOOKAHEAD)
            s_scr = s_scrs[hp % len(s_scrs)]
            m_prev2, l_prev2 = m_scr[hp], l_scr[hp]
            m_new2 = []
            for par in range(2):
                h = 2 * hp + par
                lanes = slice(par * tq, (par + 1) * tq)
                s_scr[:, lanes] = s_scr[:, lanes] + slopes[h] * g
                m_new2.append(jnp.maximum(m_prev2[:, lanes], jnp.max(s_scr[:, lanes], axis=0, keepdims=True)))
            m_new2 = jnp.concatenate(m_new2, axis=1)
            alpha2 = jnp.exp2(m_prev2 - m_new2)
            p2 = jnp.exp2(s_scr[...] - m_new2).astype(BF16)
            m_scr[hp] = m_new2
            vt = vt_ref[hp * VT_ROWS:(hp + 1) * VT_ROWS, pl.ds(start, tk)]
            pv = jnp.dot(vt, p2, preferred_element_type=F32)
            l_scr[hp] = alpha2 * l_prev2 + pv[PAIR:PAIR + 1]
            acc_old = acc_scr[hp]
            acc_scr[hp] = jnp.where(lower, alpha2[:, :tq] * acc_old + pv[:PAIR, :tq],
                                    alpha2[:, tq:] * acc_old + pv[:PAIR, tq:])
        return carry

    lax.fori_loop(0, n_tiles, attend_tile, 0)

    for hp in range(N_PAIRS):
        l2 = l_scr[hp]
        inv = jnp.where(lower, 1.0 / l2[:, :tq], 1.0 / l2[:, tq:])
        o_ref[:, hp * PAIR:(hp + 1) * PAIR] = (acc_scr[hp] * inv).T.astype(o_ref.dtype)


def _dsa(qt, qit, wit, k, vt, ki, *, n_batch, n_q, l_pad, tq, n_q_valid, q_off, n_keys, n_sel, resident):
    tk = next(t for t in (512, 256) if l_pad % t == 0 and t * tq <= DSA_TILE_ELEMS)
    assert tk % TOPK_MAX == 0 and n_sel <= TOPK_MAX
    n_qb = n_q // tq
    slopes = tuple(float(np.float32(2.0 ** (-8.0 * (h + 1) / N_HEADS))) * LOG2E for h in range(N_HEADS))
    idx_scale = float(IDX_DIM ** -0.5 * IDX_HEADS ** -0.5)
    mode = dict(pipeline_mode=pl.Buffered(1)) if resident else {}

    def qspec(w):
        return pl.BlockSpec((w, tq), lambda b, i: (0, b * n_qb + i))

    kern = functools.partial(_dsa_kernel, tq=tq, tk=tk, n_q_valid=n_q_valid, q_off=q_off, n_keys=n_keys,
                             n_sel=n_sel, slopes=slopes, idx_scale=idx_scale)
    return pl.pallas_call(
        kern,
        grid=(n_batch, n_qb),
        in_specs=[qspec(D_MODEL), qspec(IDX_HEADS * IDX_DIM), qspec(IDX_HEADS),
                  pl.BlockSpec((l_pad, D_MODEL), lambda b, i: (b, 0), **mode),
                  pl.BlockSpec((N_PAIRS * VT_ROWS, l_pad), lambda b, i: (0, b), **mode),
                  pl.BlockSpec((l_pad, IDX_DIM), lambda b, i: (b, 0), **mode)],
        out_specs=pl.BlockSpec((tq, D_MODEL), lambda b, i: (b * n_qb + i, 0)),
        out_shape=jax.ShapeDtypeStruct((n_batch * n_q, D_MODEL), BF16),
        scratch_shapes=[pltpu.VMEM((l_pad, tq), jnp.int32),
                        pltpu.VMEM((TOPK_MAX, tq), jnp.int32),
                        pltpu.VMEM((1, tq), jnp.int32),
                        pltpu.VMEM((N_PAIRS, PAIR, 2 * tq), BF16),
                        pltpu.VMEM((IDX_HEADS // 2, IDX_DIM, 2 * tq), BF16),
                        pltpu.VMEM((N_PAIRS, 1, 2 * tq), F32),
                        pltpu.VMEM((N_PAIRS, 1, 2 * tq), F32),
                        pltpu.VMEM((N_PAIRS, PAIR, tq), F32)]
                       + [pltpu.VMEM((tk, 2 * tq), F32)] * (SCORE_LOOKAHEAD + 1),
        compiler_params=_params(("parallel", "arbitrary")),
    )(qt, qit, wit, k, vt, ki)


def kernel(x_prompt, x_sample, cache_a_k, cache_a_v, cache_b_k, cache_b_v, cache_b_kidx,
           w_a_qkv, w_a_o, a_rel_bias, w_b_in, w_b_o, w_ff1, w_ff2, ln1_g, ln1_b, ln2_g, ln2_b):
    bsz, seq, _ = x_prompt.shape
    dbsz, dseq, _ = x_sample.shape
    depth = w_ff1.shape[0]
    past = cache_b_k.shape[2]
    alpha = float((2.0 * depth) ** 0.25)
    n_p = bsz * seq
    n_s = dbsz * dseq
    q_scale = float(HEAD_DIM ** -0.5) * LOG2E

    x = jnp.concatenate([x_prompt.reshape(n_p, D_MODEL), x_sample.reshape(n_s, D_MODEL)], axis=0)

    def split(a, width):
        return a[:n_p].reshape(bsz, seq, width), a[n_p:].reshape(dbsz, dseq, width)

    def heads(a):
        return a.reshape(a.shape[:-1] + (N_HEADS, HEAD_DIM))

    ak_p, av_p, ak_s, av_s = [], [], [], []
    bk_p, bv_p, bi_p, bk_s, bv_s, bi_s = [], [], [], [], [], []
    for i in range(depth):
        j = i // 2
        if i % 2 == 0:
            groups = [(0, D_MODEL, [(0, q_scale, "rows"), (1, q_scale, "cols")]),
                      (D_MODEL, D_MODEL, [(2, 1.0, "rows_head"), (3, 1.0, "rows_tail"), (4, 1.0, "rows")]),
                      (2 * D_MODEL, D_MODEL, [(5, 1.0, "rows_head"), (6, 1.0, "rows_tail"), (7, 1.0, "rows"),
                                              (8, 1.0, "pair_cols")])]
            defs = [(D_MODEL, BF16, "rows"), (D_MODEL, BF16, "cols"),
                    (D_MODEL, F32, "rows_head"), (D_MODEL, F32, "rows_tail"), (D_MODEL, BF16, "rows"),
                    (D_MODEL, F32, "rows_head"), (D_MODEL, F32, "rows_tail"), (D_MODEL, BF16, "rows"),
                    (N_PAIRS * VT_ROWS, BF16, "pair_cols")]
            q, qt, k32_p, k32_s, k16, v32_p, v32_s, v16, vt = _proj(x, w_a_qkv[j], groups, defs, n_p)
            o_p = _band_prompt(qt, k16, vt, a_rel_bias[j], n_batch=bsz, seq=seq)
            q_s = q[n_p:].reshape(dbsz, dseq, D_MODEL)
            k_s = k16[n_p:].reshape(dbsz, dseq, D_MODEL)
            v_s = v16[n_p:].reshape(dbsz, dseq, D_MODEL)
            a_len = cache_a_k.shape[2]
            ck = cache_a_k[j].reshape(dbsz, a_len, D_MODEL).astype(BF16)
            cv = cache_a_v[j].reshape(dbsz, a_len, D_MODEL).astype(BF16)
            o_s = _band_sample(q_s, k_s, v_s, ck, cv, a_rel_bias[j], past).reshape(n_s, D_MODEL)
            n_keep = min(N_PREV_CHUNKS * CHUNK, seq)
            ak_p.append(heads(k32_p.reshape(bsz, seq, D_MODEL)[:, seq - n_keep:]))
            av_p.append(heads(v32_p.reshape(bsz, seq, D_MODEL)[:, seq - n_keep:]))
            ak_s.append(heads(k32_s.reshape(dbsz, dseq, D_MODEL)))
            av_s.append(heads(v32_s.reshape(dbsz, dseq, D_MODEL)))
            w_o = w_a_o[j]
        else:
            w_in = w_b_in[j]
            w_cat = jnp.concatenate(
                [w_in[:, :3 * D_MODEL], jnp.pad(w_in[:, 3 * D_MODEL:], ((0, 0), (0, IDX_PAD - IDX_WIDTH)))],
                axis=1)
            groups = [(0, D_MODEL, [(0, q_scale, "cols")]),
                      (D_MODEL, D_MODEL, [(1, 1.0, "rows_head"), (2, 1.0, "rows_tail"), (3, 1.0, "rows")]),
                      (2 * D_MODEL, D_MODEL, [(4, 1.0, "rows_head"), (5, 1.0, "rows_tail"), (6, 1.0, "pair_cols")]),
                      (3 * D_MODEL, IDX_PAD, [(7, 1.0, "rows"), (8, 1.0, "cols")])]
            defs = [(D_MODEL, BF16, "cols"),
                    (D_MODEL, F32, "rows_head"), (D_MODEL, F32, "rows_tail"), (D_MODEL, BF16, "rows"),
                    (D_MODEL, F32, "rows_head"), (D_MODEL, F32, "rows_tail"), (N_PAIRS * VT_ROWS, BF16, "pair_cols"),
                    (IDX_PAD, F32, "rows"), (IDX_PAD, F32, "cols")]
            qt, k32_p, k32_s, k16, v32_p, v32_s, vt, idx, idxt = _proj(x, w_cat, groups, defs, n_p)
            n_qi = IDX_HEADS * IDX_DIM
            qit = idxt[:n_qi].astype(BF16)
            wit = idxt[n_qi + IDX_DIM:IDX_WIDTH]
            ki32 = idx[:, n_qi:n_qi + IDX_DIM]
            ki16 = ki32.astype(BF16)

            tq_p = _row_tile(seq, DSA_PROMPT_QUERIES)
            o_p = _dsa(qt, qit, wit, k16, vt, ki16, n_batch=bsz, n_q=seq, l_pad=seq, tq=tq_p, n_q_valid=tq_p,
                       q_off=0, n_keys=seq, n_sel=min(TOPK_MAX, seq // 4), resident=True)
            tq = 2 * CHUNK

            n_keys = past + dseq
            l_pad = -(-n_keys // TOPK_MAX) * TOPK_MAX
            tq_s = -(-dseq // tq) * tq

            def pad_q(a):
                w = a.shape[0]
                a = jnp.pad(a[:, n_p:].reshape(w, dbsz, dseq), ((0, 0), (0, 0), (0, tq_s - dseq)))
                return a.reshape(w, dbsz * tq_s)

            def keys_rows(cache, new):
                w = new.shape[-1]
                a = jnp.concatenate([cache.astype(BF16), new.reshape(dbsz, dseq, w)], axis=1)
                return jnp.pad(a, ((0, 0), (0, l_pad - n_keys), (0, 0))).reshape(dbsz * l_pad, w)

            k_all = keys_rows(cache_b_k[j].reshape(dbsz, past, D_MODEL), k16[n_p:])
            ki_all = keys_rows(cache_b_kidx[j], ki16[n_p:])
            cvt = jnp.transpose(cache_b_v[j].reshape(dbsz, past, N_PAIRS, PAIR).astype(BF16), (2, 3, 0, 1))
            cvt = jnp.concatenate([cvt, jnp.ones((N_PAIRS, ONES_ROWS, dbsz, past), BF16)], axis=1)
            vt_all = jnp.concatenate([cvt.reshape(N_PAIRS * VT_ROWS, dbsz, past),
                                      vt[:, n_p:].reshape(N_PAIRS * VT_ROWS, dbsz, dseq)], axis=2)
            vt_all = jnp.pad(vt_all, ((0, 0), (0, 0), (0, l_pad - n_keys))).reshape(N_PAIRS * VT_ROWS, dbsz * l_pad)
            o_s = _dsa(pad_q(qt), pad_q(qit), pad_q(wit), k_all, vt_all, ki_all, n_batch=dbsz, n_q=tq_s,
                       l_pad=l_pad, tq=tq, n_q_valid=dseq, q_off=past, n_keys=n_keys,
                       n_sel=min(TOPK_MAX, n_keys // 4), resident=False)
            o_s = o_s.reshape(dbsz, tq_s, D_MODEL)[:, :dseq].reshape(n_s, D_MODEL)

            ki_p, ki_s = split(ki32, IDX_DIM)
            bk_p.append(heads(k32_p.reshape(bsz, seq, D_MODEL))); bv_p.append(heads(v32_p.reshape(bsz, seq, D_MODEL)))
            bk_s.append(heads(k32_s.reshape(dbsz, dseq, D_MODEL))); bv_s.append(heads(v32_s.reshape(dbsz, dseq, D_MODEL)))
            bi_p.append(ki_p); bi_s.append(ki_s)
            w_o = w_b_o[j]

        o = jnp.concatenate([o_p, o_s], axis=0)
        x = _oproj_ln(o, w_o, x, ln1_g[i], ln1_b[i], alpha)
        x = _ffn_ln(x, w_ff1[i], w_ff2[i], ln2_g[i], ln2_b[i], alpha)

    y_p, y_s = split(x, D_MODEL)
    return (y_p, y_s,
            jnp.stack(ak_p), jnp.stack(av_p), jnp.stack(bk_p), jnp.stack(bv_p), jnp.stack(bi_p),
            jnp.stack(ak_s), jnp.stack(av_s), jnp.stack(bk_s), jnp.stack(bv_s), jnp.stack(bi_s))
```

```python
import functools
import math

import numpy as np
import jax
import jax.numpy as jnp
from jax import lax
from jax.experimental import pallas as pl
from jax.experimental.pallas import tpu as pltpu

D_MODEL = 1024
N_HEADS = 16
HEAD_DIM = 64
PAIR = 2 * HEAD_DIM
N_PAIRS = N_HEADS // 2
CHUNK = 64
N_PREV_CHUNKS = 8
REL_CLIP = 128
IDX_HEADS = 8
IDX_DIM = 64
IDX_WIDTH = IDX_HEADS * IDX_DIM + IDX_DIM + IDX_HEADS
IDX_PAD = 640
TOPK_MAX = 256
LN_EPS = 1e-5
LOG2E = math.log2(math.e)

NEG = -1e30
M_INIT = -3e38
SCORE_LOOKAHEAD = 3
DSA_PROMPT_QUERIES = 128
DSA_TILE_ELEMS = 512 * 128
ONES_ROWS = 16
VT_ROWS = PAIR + ONES_ROWS
KEY_NEG_INF = -2139095041
INT_MIN = -2147483648
INT_MAX = 2147483647
V7X_VMEM_LIMIT = 60 * 1024 * 1024

F32 = jnp.float32
BF16 = jnp.bfloat16


def _params(sem, flags=None):
    return pltpu.CompilerParams(dimension_semantics=sem, vmem_limit_bytes=V7X_VMEM_LIMIT, flags=flags)


def _row_tile(m, cap):
    t = cap
    while m % t:
        t //= 2
    return t


def _const_spec(shape):
    nd = len(shape)
    return pl.BlockSpec(shape, lambda *_: (0,) * nd, pipeline_mode=pl.Buffered(1))


def _proj_kernel(x_ref, w_ref, *out_refs, groups, n_head_blocks):
    i = pl.program_id(0)
    xb = x_ref[...].astype(BF16)
    for col, width, outs in groups:
        r = jnp.dot(xb, w_ref[:, col:col + width].astype(BF16), preferred_element_type=F32)
        for idx, scale, layout in outs:
            o = out_refs[idx]
            v = r * scale if scale != 1.0 else r
            if layout == "rows":
                o[...] = v.astype(o.dtype)
            elif layout in ("rows_head", "rows_tail"):
                @pl.when((i < n_head_blocks) if layout == "rows_head" else (i >= n_head_blocks))
                def _(o=o, v=v):
                    o[...] = v.astype(o.dtype)
            elif layout == "cols":
                o[...] = v.T.astype(o.dtype)
            else:
                vt = v.T.astype(o.dtype)
                for hp in range(N_PAIRS):
                    o[hp * VT_ROWS:hp * VT_ROWS + PAIR, :] = vt[hp * PAIR:(hp + 1) * PAIR]
                    o[hp * VT_ROWS + PAIR:(hp + 1) * VT_ROWS, :] = jnp.ones((ONES_ROWS, vt.shape[1]), o.dtype)


def _proj(x, w, groups, out_defs, n_head):
    m = x.shape[0]
    tm = _row_tile(math.gcd(m, n_head), 512)
    nhb = n_head // tm
    shapes = {"rows": lambda wd: (m, wd), "rows_head": lambda wd: (n_head, wd), "rows_tail": lambda wd: (m - n_head, wd),
              "cols": lambda wd: (wd, m), "pair_cols": lambda wd: (wd, m)}
    specs = {"rows": lambda wd: pl.BlockSpec((tm, wd), lambda i: (i, 0)),
             "rows_head": lambda wd: pl.BlockSpec((tm, wd), lambda i: (jnp.minimum(i, nhb - 1), 0)),
             "rows_tail": lambda wd: pl.BlockSpec((tm, wd), lambda i: (jnp.maximum(i - nhb, 0), 0)),
             "cols": lambda wd: pl.BlockSpec((wd, tm), lambda i: (0, i)),
             "pair_cols": lambda wd: pl.BlockSpec((wd, tm), lambda i: (0, i))}
    out_shape = [jax.ShapeDtypeStruct(shapes[lay](wd), dt) for wd, dt, lay in out_defs]
    out_specs = [specs[lay](wd) for wd, _, lay in out_defs]
    return pl.pallas_call(
        functools.partial(_proj_kernel, groups=groups, n_head_blocks=nhb),
        grid=(m // tm,),
        in_specs=[pl.BlockSpec((tm, D_MODEL), lambda i: (i, 0)), _const_spec(w.shape)],
        out_specs=out_specs,
        out_shape=out_shape,
        compiler_params=_params(("arbitrary",)),
    )(x, w)


def _deepnorm_ln(x, r, g, b, alpha):
    y = alpha * x + r
    mu = jnp.mean(y, axis=-1, keepdims=True)
    yc = y - mu
    var = jnp.mean(yc * yc, axis=-1, keepdims=True)
    return yc * lax.rsqrt(var + LN_EPS) * g + b


def _oproj_ln_kernel(o_ref, w_ref, x_ref, g_ref, b_ref, out_ref, *, alpha):
    r = jnp.dot(o_ref[...], w_ref[...].astype(BF16), preferred_element_type=F32)
    out_ref[...] = _deepnorm_ln(x_ref[...], r, g_ref[...], b_ref[...], alpha)


def _oproj_ln(o, w, x, g, b, alpha):
    m = x.shape[0]
    tm = _row_tile(m, 1024)
    row = pl.BlockSpec((tm, D_MODEL), lambda i: (i, 0))
    return pl.pallas_call(
        functools.partial(_oproj_ln_kernel, alpha=alpha),
        grid=(m // tm,),
        in_specs=[row, _const_spec(w.shape), row, _const_spec((1, D_MODEL)), _const_spec((1, D_MODEL))],
        out_specs=row,
        out_shape=jax.ShapeDtypeStruct((m, D_MODEL), F32),
        compiler_params=_params(("parallel",)),
    )(o, w, x, g.reshape(1, D_MODEL), b.reshape(1, D_MODEL))


def _ffn_ln_kernel(x_ref, w1_ref, w2_ref, g_ref, b_ref, out_ref, acc_ref, *, alpha):
    f = pl.program_id(1)

    @pl.when(f == 0)
    def _():
        acc_ref[...] = jnp.zeros_like(acc_ref)

    h = jnp.dot(x_ref[...].astype(BF16), w1_ref[...].astype(BF16), preferred_element_type=F32)
    h = jnp.maximum(h, 0.0)
    acc_ref[...] += jnp.dot((h * h).astype(BF16), w2_ref[...].astype(BF16), preferred_element_type=F32)

    @pl.when(f == pl.num_programs(1) - 1)
    def _():
        out_ref[...] = _deepnorm_ln(x_ref[...], acc_ref[...], g_ref[...], b_ref[...], alpha)


def _ffn_ln(x, w1, w2, g, b, alpha):
    m = x.shape[0]
    d_ff = w1.shape[1]
    tm = _row_tile(m, 1024)
    tf = _row_tile(d_ff, 1024)
    row = pl.BlockSpec((tm, D_MODEL), lambda i, f: (i, 0))
    return pl.pallas_call(
        functools.partial(_ffn_ln_kernel, alpha=alpha),
        grid=(m // tm, d_ff // tf),
        in_specs=[row,
                  pl.BlockSpec((D_MODEL, tf), lambda i, f: (0, f)),
                  pl.BlockSpec((tf, D_MODEL), lambda i, f: (f, 0)),
                  _const_spec((1, D_MODEL)), _const_spec((1, D_MODEL))],
        out_specs=row,
        out_shape=jax.ShapeDtypeStruct((m, D_MODEL), F32),
        scratch_shapes=[pltpu.VMEM((tm, D_MODEL), F32)],
        compiler_params=_params(("parallel", "arbitrary")),
    )(x, w1, w2, g.reshape(1, D_MODEL), b.reshape(1, D_MODEL))


def _band_kernel(*refs, seg_counts, n_lead_invalid_fn, tq):
    n_seg = sum(seg_counts)
    q_ref = refs[0]
    k_refs = refs[1:1 + n_seg]
    v_refs = refs[1 + n_seg:1 + 2 * n_seg]
    bias_refs = refs[1 + 2 * n_seg:1 + 2 * n_seg + len(seg_counts)]
    o_ref = refs[-1]

    lane = lax.broadcasted_iota(jnp.int32, (tq, PAIR), 1)
    upper = lane >= HEAD_DIM
    for hp in range(N_PAIRS):
        cols = slice(hp * PAIR, (hp + 1) * PAIR)
        qp = q_ref[0, :, cols]
        kg, vg = [], []
        s0 = 0
        for cnt in seg_counts:
            ks = [k_refs[s0 + t][0, :, cols] for t in range(cnt)]
            vs = [v_refs[s0 + t][0, :, cols] for t in range(cnt)]
            kg.append(ks[0] if cnt == 1 else jnp.concatenate(ks, axis=0))
            vg.append(vs[0] if cnt == 1 else jnp.concatenate(vs, axis=0))
            s0 += cnt
        outs = []
        for par in range(2):
            h = 2 * hp + par
            qm = jnp.where(upper == (par == 1), qp, jnp.zeros_like(qp))
            s = []
            for g, kmat in enumerate(kg):
                sg = lax.dot_general(qm, kmat, (((1,), (1,)), ((), ())), preferred_element_type=F32)
                sg = sg + bias_refs[g][h]
                if g == 0 and n_lead_invalid_fn is not None:
                    col = lax.broadcasted_iota(jnp.int32, sg.shape, 1)
                    sg = jnp.where(col >= n_lead_invalid_fn(pl.program_id(1)), sg, NEG)
                s.append(sg)
            m = functools.reduce(jnp.maximum, [jnp.max(sg, axis=1, keepdims=True) for sg in s])
            p = [jnp.exp2(sg - m) for sg in s]
            l = functools.reduce(jnp.add, [jnp.sum(pg, axis=1, keepdims=True) for pg in p])
            o = functools.reduce(jnp.add, [jnp.dot(pg.astype(BF16), vmat, preferred_element_type=F32)
                                           for pg, vmat in zip(p, vg)])
            outs.append(o / l)
        o_ref[0, :, cols] = jnp.where(upper, outs[1], outs[0]).astype(o_ref.dtype)


def _band_bias(tab, q0, n_q, k0, n_k):
    span = n_q + n_k - 1
    rel = (q0 - k0) + (n_q - 1) - jnp.arange(span)
    u = tab[:, jnp.clip(rel, -REL_CLIP, REL_CLIP) + REL_CLIP].astype(F32) * LOG2E
    n_h = tab.shape[0]
    x = jnp.broadcast_to(jnp.pad(u, ((0, 0), (0, 1)))[:, None, :], (n_h, n_q, span + 1))
    x = x.reshape(n_h, n_q * (span + 1))[:, :n_q * span].reshape(n_h, n_q, span)
    bias = x[:, :, n_q - 1:n_q - 1 + n_k]
    qch = (q0 + jnp.arange(n_q))[:, None] // CHUNK
    kch = (k0 + jnp.arange(n_k))[None, :] // CHUNK
    valid = (kch <= qch) & (kch >= qch - N_PREV_CHUNKS)
    return jnp.where(valid[None], bias, NEG)


def _band_prompt_kernel(qt_ref, *refs, n_seg, tq):
    k_refs = refs[:n_seg]
    vt_refs = refs[n_seg:2 * n_seg]
    bias_ref, o_ref = refs[2 * n_seg], refs[2 * n_seg + 1]
    s_scrs = refs[2 * n_seg + 2:]
    n_keys = n_seg * tq
    n_invalid = ((n_seg - 1) - pl.program_id(1)) * tq
    start_mask = jnp.where(lax.broadcasted_iota(jnp.int32, (n_keys, tq), 0) >= n_invalid, 0.0, NEG)
    lower = lax.broadcasted_iota(jnp.int32, (PAIR, tq), 0) < HEAD_DIM

    def issue_scores(hp):
        cols = slice(hp * PAIR, (hp + 1) * PAIR)
        kmat = jnp.concatenate([r[:, cols] for r in k_refs], axis=0)
        qt = qt_ref[cols, :]
        q2t = jnp.concatenate([jnp.where(lower, qt, jnp.zeros_like(qt)),
                               jnp.where(lower, jnp.zeros_like(qt), qt)], axis=1)
        s_scrs[hp % len(s_scrs)][...] = jnp.dot(kmat, q2t, preferred_element_type=F32)

    for hp in range(SCORE_LOOKAHEAD):
        issue_scores(hp)
    for hp in range(N_PAIRS):
        if hp + SCORE_LOOKAHEAD < N_PAIRS:
            issue_scores(hp + SCORE_LOOKAHEAD)
        s_scr = s_scrs[hp % len(s_scrs)]
        ps = []
        for par in range(2):
            s = s_scr[:, par * tq:(par + 1) * tq] + bias_ref[2 * hp + par] + start_mask
            ps.append(jnp.exp2(s - jnp.max(s, axis=0, keepdims=True)).astype(BF16))
        vt = jnp.concatenate([r[hp * VT_ROWS:(hp + 1) * VT_ROWS, :] for r in vt_refs], axis=1)
        pv = jnp.dot(vt, jnp.concatenate(ps, axis=1), preferred_element_type=F32)
        o_t = jnp.where(lower, pv[:PAIR, :tq] / pv[PAIR:PAIR + 1, :tq], pv[:PAIR, tq:] / pv[PAIR:PAIR + 1, tq:])
        o_ref[:, hp * PAIR:(hp + 1) * PAIR] = o_t.T.astype(o_ref.dtype)


def _band_prompt(qt, k, vt, tab, *, n_batch, seq):
    tq = 2 * CHUNK
    n_seg = (N_PREV_CHUNKS * CHUNK) // tq + 1
    n_qb = seq // tq
    bias_t = jnp.swapaxes(_band_bias(tab, (n_seg - 1) * tq, tq, 0, n_seg * tq), 1, 2)

    def seg_block(b, i, s):
        return b * n_qb + jnp.maximum(i - (n_seg - 1) + s, 0)

    k_specs = [pl.BlockSpec((tq, D_MODEL), lambda b, i, s=s: (seg_block(b, i, s), 0)) for s in range(n_seg)]
    vt_specs = [pl.BlockSpec((N_PAIRS * VT_ROWS, tq), lambda b, i, s=s: (0, seg_block(b, i, s)))
                for s in range(n_seg)]
    return pl.pallas_call(
        functools.partial(_band_prompt_kernel, n_seg=n_seg, tq=tq),
        grid=(n_batch, n_qb),
        in_specs=[pl.BlockSpec((D_MODEL, tq), lambda b, i: (0, b * n_qb + i))] + k_specs + vt_specs
                 + [_const_spec(bias_t.shape)],
        out_specs=pl.BlockSpec((tq, D_MODEL), lambda b, i: (b * n_qb + i, 0)),
        out_shape=jax.ShapeDtypeStruct((n_batch * seq, D_MODEL), BF16),
        scratch_shapes=[pltpu.VMEM((n_seg * tq, 2 * tq), F32)] * (SCORE_LOOKAHEAD + 1),
        compiler_params=_params(("parallel", "parallel")),
    )(qt, *([k] * n_seg), *([vt] * n_seg), bias_t)


def _band_sample(q, k_new, v_new, cache_k, cache_v, tab, past):
    bsz, s_len, _ = q.shape
    a_len = cache_k.shape[1]
    bias_c = _band_bias(tab, past, s_len, past - a_len, a_len)
    bias_n = _band_bias(tab, past, s_len, past, s_len)

    def spec(n):
        return pl.BlockSpec((1, n, D_MODEL), lambda b, i: (b, 0, 0))

    kern = functools.partial(_band_kernel, seg_counts=(1, 1), tq=s_len, n_lead_invalid_fn=None)
    return pl.pallas_call(
        kern,
        grid=(bsz, 1),
        in_specs=[spec(s_len), spec(a_len), spec(s_len), spec(a_len), spec(s_len),
                  _const_spec(bias_c.shape), _const_spec(bias_n.shape)],
        out_specs=spec(s_len),
        out_shape=jax.ShapeDtypeStruct(q.shape, BF16),
        compiler_params=_params(("parallel", "arbitrary")),
    )(q, cache_k, k_new, cache_v, v_new, bias_c, bias_n)


def _dsa_kernel(qt_ref, qit_ref, wit_ref, k_ref, vt_ref, ki_ref, o_ref,
                key_scr, gmax_scr, cut_scr, q2t_scr, qi2t_scr, m_scr, l_scr, acc_scr, *s_scrs,
                tq, tk, n_q_valid, q_off, n_keys, n_sel, slopes, idx_scale):
    i = pl.program_id(1)
    q_pos0 = q_off + i * tq
    last_chunk = (q_pos0 + tq - 1) // CHUNK
    n_adm = jnp.minimum((last_chunk + 1) * CHUNK, n_keys)
    n_tiles = (n_adm + tk - 1) // tk

    q_lane = lax.broadcasted_iota(jnp.int32, (1, tq), 1)
    q_pos = q_pos0 + q_lane
    adm_end = jnp.minimum((q_pos // CHUNK + 1) * CHUNK, n_keys)

    def tile_start(j):
        return pl.multiple_of(j * tk, tk)

    def key_pos(j):
        return j * tk + lax.broadcasted_iota(jnp.int32, (tk, tq), 0)

    wit = wit_ref[...] * idx_scale

    for h2 in range(IDX_HEADS // 2):
        qi2t_scr[h2, :, 0:tq] = qit_ref[(2 * h2) * IDX_DIM:(2 * h2 + 1) * IDX_DIM, :]
        qi2t_scr[h2, :, tq:2 * tq] = qit_ref[(2 * h2 + 1) * IDX_DIM:(2 * h2 + 2) * IDX_DIM, :]

    assert len(s_scrs) >= IDX_HEADS // 2

    def issue_index_logits(j, h2):
        ki = ki_ref[pl.ds(tile_start(jnp.minimum(j, n_tiles - 1)), tk), :]
        lg = jnp.dot(ki, qi2t_scr[h2], preferred_element_type=F32)
        s_scrs[h2][0] = lg[:, :tq]
        s_scrs[h2][1] = lg[:, tq:]

    for h2 in range(IDX_HEADS // 2):
        issue_index_logits(0, h2)

    def score_tile(j, carry):
        acc = jnp.zeros((tk, tq), F32)
        for h2 in range(IDX_HEADS // 2):
            acc = (acc + wit[2 * h2:2 * h2 + 1, :] * jnp.maximum(s_scrs[h2][0], 0.0)
                   + wit[2 * h2 + 1:2 * h2 + 2, :] * jnp.maximum(s_scrs[h2][1], 0.0))
            issue_index_logits(j + 1, h2)
        acc = jnp.where(acc == 0.0, 0.0, acc)
        bits = pltpu.bitcast(acc, jnp.int32)
        key = bits ^ ((bits >> 31) & INT_MAX)
        key = jnp.where(key_pos(j) < adm_end, key, KEY_NEG_INF)
        key_scr[pl.ds(tile_start(j), tk), :] = key
        gmax_scr[...] = jnp.maximum(gmax_scr[...], key.reshape(tk // TOPK_MAX, TOPK_MAX, tq).max(axis=0))
        return carry

    gmax_scr[...] = jnp.full(gmax_scr.shape, INT_MIN, jnp.int32)
    lax.fori_loop(0, n_tiles, score_tile, 0)

    def count(pred):
        def body(j, cnt):
            kt = key_scr[pl.ds(tile_start(j), tk), :]
            hit = pred(kt, j).astype(jnp.int32)
            return cnt + hit.reshape(tk // 8, 8, tq).sum(axis=0)
        cnt = lax.fori_loop(0, n_tiles, body, jnp.zeros((8, tq), jnp.int32))
        return jnp.sum(cnt, axis=0, keepdims=True)

    gmax = gmax_scr[...]
    lo0 = jnp.min(gmax, axis=0, keepdims=True)
    top = jnp.max(gmax, axis=0, keepdims=True)
    hi0 = jnp.where(top == INT_MAX, top, top + 1)
    n_halvings = jnp.max(32 - lax.clz(hi0 - lo0 - 1))

    def thr_step(_, carry):
        lo, hi = carry
        mid = lo + lax.shift_right_logical(hi - lo, 1)
        ok = count(lambda kt, j: kt >= mid) >= n_sel
        return jnp.where(ok, mid, lo), jnp.where(ok, hi, mid)

    thr, _ = lax.fori_loop(0, n_halvings, thr_step, (lo0, hi0))
    cnt_thr = count(lambda kt, j: kt >= thr)

    tie = (cnt_thr > n_sel) & (q_lane < n_q_valid)
    cut_scr[...] = jnp.full((1, tq), INT_MAX, jnp.int32)

    @pl.when(jnp.max(tie.astype(jnp.int32)) > 0)
    def _():
        need = n_sel - count(lambda kt, j: kt > thr)
        n_bits = int(key_scr.shape[0]).bit_length()

        def pos_step(b, cut):
            cand = cut + lax.shift_left(jnp.int32(1), n_bits - 1 - b)
            c = count(lambda kt, j: (kt == thr) & (key_pos(j) < cand))
            return jnp.where(c < need, cand, cut)

        cut = lax.fori_loop(0, n_bits, pos_step, jnp.zeros((1, tq), jnp.int32))
        cut_scr[...] = jnp.where(tie, cut, INT_MAX)

    cut = cut_scr[...]

    lower = lax.broadcasted_iota(jnp.int32, (PAIR, tq), 0) < HEAD_DIM
    for hp in range(N_PAIRS):
        qt = qt_ref[hp * PAIR:(hp + 1) * PAIR, :]
        q2t_scr[hp, :, 0:tq] = jnp.where(lower, qt, jnp.zeros_like(qt))
        q2t_scr[hp, :, tq:2 * tq] = jnp.where(lower, jnp.zeros_like(qt), qt)
    m_scr[...] = jnp.full(m_scr.shape, M_INIT, F32)
    l_scr[...] = jnp.zeros(l_scr.shape, F32)
    acc_scr[...] = jnp.zeros(acc_scr.shape, F32)
    thr_adm = jnp.maximum(thr, KEY_NEG_INF)
    cut_adm = jnp.where(thr > KEY_NEG_INF, cut, -1)

    def attend_tile(j, carry):
        start = tile_start(j)

        def issue_scores(hp):
            kmat = k_ref[pl.ds(start, tk), hp * PAIR:(hp + 1) * PAIR]
            s2 = jnp.dot(kmat, q2t_scr[hp], preferred_element_type=F32)
            s_scrs[hp % len(s_scrs)][0] = s2[:, :tq]
            s_scrs[hp % len(s_scrs)][1] = s2[:, tq:]

        for hp in range(SCORE_LOOKAHEAD):
            issue_scores(hp)
        kt = key_scr[pl.ds(start, tk), :]
        kp = key_pos(j)
        sel = (kt > thr_adm) | ((kt == thr) & (kp <= cut_adm))
        g = jnp.where(sel, -jnp.abs(kp - q_pos).astype(F32), NEG)
        for hp in range(N_PAIRS):
            if hp + SCORE_LOOKAHEAD < N_PAIRS:
                issue_scores(hp + SCORE_LOOKAHEAD)
            s_scr = s_scrs[hp % len(s_scrs)]
            m_prev2, l_prev2 = m_scr[hp], l_scr[hp]
            tops = []
            for par in range(2):
                s_scr[par] = s_scr[par] + slopes[2 * hp + par] * g
                tops.append(jnp.max(s_scr[par], axis=0, keepdims=True))
            m_new2 = jnp.maximum(m_prev2, jnp.concatenate(tops, axis=1))
            alpha2 = jnp.exp2(m_prev2 - m_new2)
            p2 = jnp.concatenate([jnp.exp2(s_scr[0] - m_new2[:, :tq]), jnp.exp2(s_scr[1] - m_new2[:, tq:])],
                                 axis=1).astype(BF16)
            m_scr[hp] = m_new2
            vt = vt_ref[hp * VT_ROWS:(hp + 1) * VT_ROWS, pl.ds(start, tk)]
            pv = jnp.dot(vt, p2, preferred_element_type=F32)
            l_scr[hp] = alpha2 * l_prev2 + pv[PAIR:PAIR + 1]
            acc_old = acc_scr[hp]
            acc_scr[hp] = jnp.where(lower, alpha2[:, :tq] * acc_old + pv[:PAIR, :tq],
                                    alpha2[:, tq:] * acc_old + pv[:PAIR, tq:])
        return carry

    lax.fori_loop(0, n_tiles, attend_tile, 0)

    for hp in range(N_PAIRS):
        l2 = l_scr[hp]
        inv = jnp.where(lower, 1.0 / l2[:, :tq], 1.0 / l2[:, tq:])
        o_ref[:, hp * PAIR:(hp + 1) * PAIR] = (acc_scr[hp] * inv).T.astype(o_ref.dtype)


def _dsa(qt, qit, wit, k, vt, ki, *, n_batch, n_q, l_pad, tq, n_q_valid, q_off, n_keys, n_sel, resident):
    tk = next(t for t in (512, 256) if l_pad % t == 0 and t * tq <= DSA_TILE_ELEMS)
    assert tk % TOPK_MAX == 0 and n_sel <= TOPK_MAX
    n_qb = n_q // tq
    slopes = tuple(float(np.float32(2.0 ** (-8.0 * (h + 1) / N_HEADS))) * LOG2E for h in range(N_HEADS))
    idx_scale = float(IDX_DIM ** -0.5 * IDX_HEADS ** -0.5)
    mode = dict(pipeline_mode=pl.Buffered(1)) if resident else {}

    def qspec(w):
        return pl.BlockSpec((w, tq), lambda b, i: (0, b * n_qb + i))

    kern = functools.partial(_dsa_kernel, tq=tq, tk=tk, n_q_valid=n_q_valid, q_off=q_off, n_keys=n_keys,
                             n_sel=n_sel, slopes=slopes, idx_scale=idx_scale)
    return pl.pallas_call(
        kern,
        grid=(n_batch, n_qb),
        in_specs=[qspec(D_MODEL), qspec(IDX_HEADS * IDX_DIM), qspec(IDX_HEADS),
                  pl.BlockSpec((l_pad, D_MODEL), lambda b, i: (b, 0), **mode),
                  pl.BlockSpec((N_PAIRS * VT_ROWS, l_pad), lambda b, i: (0, b), **mode),
                  pl.BlockSpec((l_pad, IDX_DIM), lambda b, i: (b, 0), **mode)],
        out_specs=pl.BlockSpec((tq, D_MODEL), lambda b, i: (b * n_qb + i, 0)),
        out_shape=jax.ShapeDtypeStruct((n_batch * n_q, D_MODEL), BF16),
        scratch_shapes=[pltpu.VMEM((l_pad, tq), jnp.int32),
                        pltpu.VMEM((TOPK_MAX, tq), jnp.int32),
                        pltpu.VMEM((1, tq), jnp.int32),
                        pltpu.VMEM((N_PAIRS, PAIR, 2 * tq), BF16),
                        pltpu.VMEM((IDX_HEADS // 2, IDX_DIM, 2 * tq), BF16),
                        pltpu.VMEM((N_PAIRS, 1, 2 * tq), F32),
                        pltpu.VMEM((N_PAIRS, 1, 2 * tq), F32),
                        pltpu.VMEM((N_PAIRS, PAIR, tq), F32)]
                       + [pltpu.VMEM((2, tk, tq), F32)] * (SCORE_LOOKAHEAD + 1),
        compiler_params=_params(("parallel", "arbitrary")),
    )(qt, qit, wit, k, vt, ki)


def kernel(x_prompt, x_sample, cache_a_k, cache_a_v, cache_b_k, cache_b_v, cache_b_kidx,
           w_a_qkv, w_a_o, a_rel_bias, w_b_in, w_b_o, w_ff1, w_ff2, ln1_g, ln1_b, ln2_g, ln2_b):
    bsz, seq, _ = x_prompt.shape
    dbsz, dseq, _ = x_sample.shape
    depth = w_ff1.shape[0]
    past = cache_b_k.shape[2]
    alpha = float((2.0 * depth) ** 0.25)
    n_p = bsz * seq
    n_s = dbsz * dseq
    q_scale = float(HEAD_DIM ** -0.5) * LOG2E

    x = jnp.concatenate([x_prompt.reshape(n_p, D_MODEL), x_sample.reshape(n_s, D_MODEL)], axis=0)

    def split(a, width):
        return a[:n_p].reshape(bsz, seq, width), a[n_p:].reshape(dbsz, dseq, width)

    def heads(a):
        return a.reshape(a.shape[:-1] + (N_HEADS, HEAD_DIM))

    ak_p, av_p, ak_s, av_s = [], [], [], []
    bk_p, bv_p, bi_p, bk_s, bv_s, bi_s = [], [], [], [], [], []
    for i in range(depth):
        j = i // 2
        if i % 2 == 0:
            groups = [(0, D_MODEL, [(0, q_scale, "rows"), (1, q_scale, "cols")]),
                      (D_MODEL, D_MODEL, [(2, 1.0, "rows_head"), (3, 1.0, "rows_tail"), (4, 1.0, "rows")]),
                      (2 * D_MODEL, D_MODEL, [(5, 1.0, "rows_head"), (6, 1.0, "rows_tail"), (7, 1.0, "rows"),
                                              (8, 1.0, "pair_cols")])]
            defs = [(D_MODEL, BF16, "rows"), (D_MODEL, BF16, "cols"),
                    (D_MODEL, F32, "rows_head"), (D_MODEL, F32, "rows_tail"), (D_MODEL, BF16, "rows"),
                    (D_MODEL, F32, "rows_head"), (D_MODEL, F32, "rows_tail"), (D_MODEL, BF16, "rows"),
                    (N_PAIRS * VT_ROWS, BF16, "pair_cols")]
            q, qt, k32_p, k32_s, k16, v32_p, v32_s, v16, vt = _proj(x, w_a_qkv[j], groups, defs, n_p)
            o_p = _band_prompt(qt, k16, vt, a_rel_bias[j], n_batch=bsz, seq=seq)
            q_s = q[n_p:].reshape(dbsz, dseq, D_MODEL)
            k_s = k16[n_p:].reshape(dbsz, dseq, D_MODEL)
            v_s = v16[n_p:].reshape(dbsz, dseq, D_MODEL)
            a_len = cache_a_k.shape[2]
            ck = cache_a_k[j].reshape(dbsz, a_len, D_MODEL).astype(BF16)
            cv = cache_a_v[j].reshape(dbsz, a_len, D_MODEL).astype(BF16)
            o_s = _band_sample(q_s, k_s, v_s, ck, cv, a_rel_bias[j], past).reshape(n_s, D_MODEL)
            n_keep = min(N_PREV_CHUNKS * CHUNK, seq)
            ak_p.append(heads(k32_p.reshape(bsz, seq, D_MODEL)[:, seq - n_keep:]))
            av_p.append(heads(v32_p.reshape(bsz, seq, D_MODEL)[:, seq - n_keep:]))
            ak_s.append(heads(k32_s.reshape(dbsz, dseq, D_MODEL)))
            av_s.append(heads(v32_s.reshape(dbsz, dseq, D_MODEL)))
            w_o = w_a_o[j]
        else:
            w_in = w_b_in[j]
            w_cat = jnp.concatenate(
                [w_in[:, :3 * D_MODEL], jnp.pad(w_in[:, 3 * D_MODEL:], ((0, 0), (0, IDX_PAD - IDX_WIDTH)))],
                axis=1)
            groups = [(0, D_MODEL, [(0, q_scale, "cols")]),
                      (D_MODEL, D_MODEL, [(1, 1.0, "rows_head"), (2, 1.0, "rows_tail"), (3, 1.0, "rows")]),
                      (2 * D_MODEL, D_MODEL, [(4, 1.0, "rows_head"), (5, 1.0, "rows_tail"), (6, 1.0, "pair_cols")]),
                      (3 * D_MODEL, IDX_PAD, [(7, 1.0, "rows"), (8, 1.0, "cols")])]
            defs = [(D_MODEL, BF16, "cols"),
                    (D_MODEL, F32, "rows_head"), (D_MODEL, F32, "rows_tail"), (D_MODEL, BF16, "rows"),
                    (D_MODEL, F32, "rows_head"), (D_MODEL, F32, "rows_tail"), (N_PAIRS * VT_ROWS, BF16, "pair_cols"),
                    (IDX_PAD, F32, "rows"), (IDX_PAD, F32, "cols")]
            qt, k32_p, k32_s, k16, v32_p, v32_s, vt, idx, idxt = _proj(x, w_cat, groups, defs, n_p)
            n_qi = IDX_HEADS * IDX_DIM
            qit = idxt[:n_qi].astype(BF16)
            wit = idxt[n_qi + IDX_DIM:IDX_WIDTH]
            ki32 = idx[:, n_qi:n_qi + IDX_DIM]
            ki16 = ki32.astype(BF16)

            tq_p = _row_tile(seq, DSA_PROMPT_QUERIES)
            o_p = _dsa(qt, qit, wit, k16, vt, ki16, n_batch=bsz, n_q=seq, l_pad=seq, tq=tq_p, n_q_valid=tq_p,
                       q_off=0, n_keys=seq, n_sel=min(TOPK_MAX, seq // 4), resident=True)
            tq = 2 * CHUNK

            n_keys = past + dseq
            l_pad = -(-n_keys // TOPK_MAX) * TOPK_MAX
            tq_s = -(-dseq // tq) * tq

            def pad_q(a):
                w = a.shape[0]
                a = jnp.pad(a[:, n_p:].reshape(w, dbsz, dseq), ((0, 0), (0, 0), (0, tq_s - dseq)))
                return a.reshape(w, dbsz * tq_s)

            def keys_rows(cache, new):
                w = new.shape[-1]
                a = jnp.concatenate([cache.astype(BF16), new.reshape(dbsz, dseq, w)], axis=1)
                return jnp.pad(a, ((0, 0), (0, l_pad - n_keys), (0, 0))).reshape(dbsz * l_pad, w)

            k_all = keys_rows(cache_b_k[j].reshape(dbsz, past, D_MODEL), k16[n_p:])
            ki_all = keys_rows(cache_b_kidx[j], ki16[n_p:])
            cvt = jnp.transpose(cache_b_v[j].reshape(dbsz, past, N_PAIRS, PAIR).astype(BF16), (2, 3, 0, 1))
            cvt = jnp.concatenate([cvt, jnp.ones((N_PAIRS, ONES_ROWS, dbsz, past), BF16)], axis=1)
            vt_all = jnp.concatenate([cvt.reshape(N_PAIRS * VT_ROWS, dbsz, past),
                                      vt[:, n_p:].reshape(N_PAIRS * VT_ROWS, dbsz, dseq)], axis=2)
            vt_all = jnp.pad(vt_all, ((0, 0), (0, 0), (0, l_pad - n_keys))).reshape(N_PAIRS * VT_ROWS, dbsz * l_pad)
            o_s = _dsa(pad_q(qt), pad_q(qit), pad_q(wit), k_all, vt_all, ki_all, n_batch=dbsz, n_q=tq_s,
                       l_pad=l_pad, tq=tq, n_q_valid=dseq, q_off=past, n_keys=n_keys,
                       n_sel=min(TOPK_MAX, n_keys // 4), resident=False)
            o_s = o_s.reshape(dbsz, tq_s, D_MODEL)[:, :dseq].reshape(n_s, D_MODEL)

            ki_p, ki_s = split(ki32, IDX_DIM)
            bk_p.append(heads(k32_p.reshape(bsz, seq, D_MODEL))); bv_p.append(heads(v32_p.reshape(bsz, seq, D_MODEL)))
            bk_s.append(heads(k32_s.reshape(dbsz, dseq, D_MODEL))); bv_s.append(heads(v32_s.reshape(dbsz, dseq, D_MODEL)))
            bi_p.append(ki_p); bi_s.append(ki_s)
            w_o = w_b_o[j]

        o = jnp.concatenate([o_p, o_s], axis=0)
        x = _oproj_ln(o, w_o, x, ln1_g[i], ln1_b[i], alpha)
        x = _ffn_ln(x, w_ff1[i], w_ff2[i], ln2_g[i], ln2_b[i], alpha)

    y_p, y_s = split(x, D_MODEL)
    return (y_p, y_s,
            jnp.stack(ak_p), jnp.stack(av_p), jnp.stack(bk_p), jnp.stack(bv_p), jnp.stack(bi_p),
            jnp.stack(ak_s), jnp.stack(av_s), jnp.stack(bk_s), jnp.stack(bv_s), jnp.stack(bi_s))
```

```python
import functools
import math

import numpy as np
import jax
import jax.numpy as jnp
from jax import lax
from jax.experimental import pallas as pl
from jax.experimental.pallas import tpu as pltpu

D_MODEL = 1024
N_HEADS = 16
HEAD_DIM = 64
PAIR = 2 * HEAD_DIM
N_PAIRS = N_HEADS // 2
CHUNK = 64
N_PREV_CHUNKS = 8
REL_CLIP = 128
IDX_HEADS = 8
IDX_DIM = 64
IDX_WIDTH = IDX_HEADS * IDX_DIM + IDX_DIM + IDX_HEADS
IDX_PAD = 640
TOPK_MAX = 256
LN_EPS = 1e-5
LOG2E = math.log2(math.e)

NEG = -1e30
M_INIT = -3e38
SCORE_LOOKAHEAD = 3
DSA_PROMPT_QUERIES = 128
DSA_TILE_ELEMS = 512 * 128
ONES_ROWS = 16
VT_ROWS = PAIR + ONES_ROWS
KEY_NEG_INF = -2139095041
INT_MIN = -2147483648
INT_MAX = 2147483647
V7X_VMEM_LIMIT = 60 * 1024 * 1024

F32 = jnp.float32
BF16 = jnp.bfloat16


def _params(sem, flags=None):
    return pltpu.CompilerParams(dimension_semantics=sem, vmem_limit_bytes=V7X_VMEM_LIMIT, flags=flags)


def _row_tile(m, cap):
    t = cap
    while m % t:
        t //= 2
    return t


def _const_spec(shape):
    nd = len(shape)
    return pl.BlockSpec(shape, lambda *_: (0,) * nd, pipeline_mode=pl.Buffered(1))


def _proj_kernel(x_ref, w_ref, *out_refs, groups, n_head_blocks):
    i = pl.program_id(0)
    xb = x_ref[...].astype(BF16)
    for col, width, outs in groups:
        r = jnp.dot(xb, w_ref[:, col:col + width].astype(BF16), preferred_element_type=F32)
        for idx, scale, layout in outs:
            o = out_refs[idx]
            v = r * scale if scale != 1.0 else r
            if layout == "rows":
                o[...] = v.astype(o.dtype)
            elif layout in ("rows_head", "rows_tail"):
                @pl.when((i < n_head_blocks) if layout == "rows_head" else (i >= n_head_blocks))
                def _(o=o, v=v):
                    o[...] = v.astype(o.dtype)
            elif layout == "cols":
                o[...] = v.T.astype(o.dtype)
            else:
                vt = v.T.astype(o.dtype)
                for hp in range(N_PAIRS):
                    o[hp * VT_ROWS:hp * VT_ROWS + PAIR, :] = vt[hp * PAIR:(hp + 1) * PAIR]
                    o[hp * VT_ROWS + PAIR:(hp + 1) * VT_ROWS, :] = jnp.ones((ONES_ROWS, vt.shape[1]), o.dtype)


def _proj(x, w, groups, out_defs, n_head):
    m = x.shape[0]
    tm = _row_tile(math.gcd(m, n_head), 512)
    nhb = n_head // tm
    shapes = {"rows": lambda wd: (m, wd), "rows_head": lambda wd: (n_head, wd), "rows_tail": lambda wd: (m - n_head, wd),
              "cols": lambda wd: (wd, m), "pair_cols": lambda wd: (wd, m)}
    specs = {"rows": lambda wd: pl.BlockSpec((tm, wd), lambda i: (i, 0)),
             "rows_head": lambda wd: pl.BlockSpec((tm, wd), lambda i: (jnp.minimum(i, nhb - 1), 0)),
             "rows_tail": lambda wd: pl.BlockSpec((tm, wd), lambda i: (jnp.maximum(i - nhb, 0), 0)),
             "cols": lambda wd: pl.BlockSpec((wd, tm), lambda i: (0, i)),
             "pair_cols": lambda wd: pl.BlockSpec((wd, tm), lambda i: (0, i))}
    out_shape = [jax.ShapeDtypeStruct(shapes[lay](wd), dt) for wd, dt, lay in out_defs]
    out_specs = [specs[lay](wd) for wd, _, lay in out_defs]
    return pl.pallas_call(
        functools.partial(_proj_kernel, groups=groups, n_head_blocks=nhb),
        grid=(m // tm,),
        in_specs=[pl.BlockSpec((tm, D_MODEL), lambda i: (i, 0)), _const_spec(w.shape)],
        out_specs=out_specs,
        out_shape=out_shape,
        compiler_params=_params(("arbitrary",)),
    )(x, w)


def _deepnorm_ln(x, r, g, b, alpha):
    y = alpha * x + r
    mu = jnp.mean(y, axis=-1, keepdims=True)
    yc = y - mu
    var = jnp.mean(yc * yc, axis=-1, keepdims=True)
    return yc * lax.rsqrt(var + LN_EPS) * g + b


def _oproj_ln_kernel(o_ref, w_ref, x_ref, g_ref, b_ref, out_ref, *, alpha):
    r = jnp.dot(o_ref[...], w_ref[...].astype(BF16), preferred_element_type=F32)
    out_ref[...] = _deepnorm_ln(x_ref[...], r, g_ref[...], b_ref[...], alpha)


def _oproj_ln(o, w, x, g, b, alpha):
    m = x.shape[0]
    tm = _row_tile(m, 1024)
    row = pl.BlockSpec((tm, D_MODEL), lambda i: (i, 0))
    return pl.pallas_call(
        functools.partial(_oproj_ln_kernel, alpha=alpha),
        grid=(m // tm,),
        in_specs=[row, _const_spec(w.shape), row, _const_spec((1, D_MODEL)), _const_spec((1, D_MODEL))],
        out_specs=row,
        out_shape=jax.ShapeDtypeStruct((m, D_MODEL), F32),
        compiler_params=_params(("parallel",)),
    )(o, w, x, g.reshape(1, D_MODEL), b.reshape(1, D_MODEL))


def _ffn_ln_kernel(x_ref, w1_ref, w2_ref, g_ref, b_ref, out_ref, acc_ref, *, alpha):
    f = pl.program_id(1)

    @pl.when(f == 0)
    def _():
        acc_ref[...] = jnp.zeros_like(acc_ref)

    h = jnp.dot(x_ref[...].astype(BF16), w1_ref[...].astype(BF16), preferred_element_type=F32)
    h = jnp.maximum(h, 0.0)
    acc_ref[...] += jnp.dot((h * h).astype(BF16), w2_ref[...].astype(BF16), preferred_element_type=F32)

    @pl.when(f == pl.num_programs(1) - 1)
    def _():
        out_ref[...] = _deepnorm_ln(x_ref[...], acc_ref[...], g_ref[...], b_ref[...], alpha)


def _ffn_ln(x, w1, w2, g, b, alpha):
    m = x.shape[0]
    d_ff = w1.shape[1]
    tm = _row_tile(m, 1024)
    tf = _row_tile(d_ff, 1024)
    row = pl.BlockSpec((tm, D_MODEL), lambda i, f: (i, 0))
    return pl.pallas_call(
        functools.partial(_ffn_ln_kernel, alpha=alpha),
        grid=(m // tm, d_ff // tf),
        in_specs=[row,
                  pl.BlockSpec((D_MODEL, tf), lambda i, f: (0, f)),
                  pl.BlockSpec((tf, D_MODEL), lambda i, f: (f, 0)),
                  _const_spec((1, D_MODEL)), _const_spec((1, D_MODEL))],
        out_specs=row,
        out_shape=jax.ShapeDtypeStruct((m, D_MODEL), F32),
        scratch_shapes=[pltpu.VMEM((tm, D_MODEL), F32)],
        compiler_params=_params(("parallel", "arbitrary")),
    )(x, w1, w2, g.reshape(1, D_MODEL), b.reshape(1, D_MODEL))


def _band_kernel(*refs, seg_counts, n_lead_invalid_fn, tq):
    n_seg = sum(seg_counts)
    q_ref = refs[0]
    k_refs = refs[1:1 + n_seg]
    v_refs = refs[1 + n_seg:1 + 2 * n_seg]
    bias_refs = refs[1 + 2 * n_seg:1 + 2 * n_seg + len(seg_counts)]
    o_ref = refs[-1]

    lane = lax.broadcasted_iota(jnp.int32, (tq, PAIR), 1)
    upper = lane >= HEAD_DIM
    for hp in range(N_PAIRS):
        cols = slice(hp * PAIR, (hp + 1) * PAIR)
        qp = q_ref[0, :, cols]
        kg, vg = [], []
        s0 = 0
        for cnt in seg_counts:
            ks = [k_refs[s0 + t][0, :, cols] for t in range(cnt)]
            vs = [v_refs[s0 + t][0, :, cols] for t in range(cnt)]
            kg.append(ks[0] if cnt == 1 else jnp.concatenate(ks, axis=0))
            vg.append(vs[0] if cnt == 1 else jnp.concatenate(vs, axis=0))
            s0 += cnt
        outs = []
        for par in range(2):
            h = 2 * hp + par
            qm = jnp.where(upper == (par == 1), qp, jnp.zeros_like(qp))
            s = []
            for g, kmat in enumerate(kg):
                sg = lax.dot_general(qm, kmat, (((1,), (1,)), ((), ())), preferred_element_type=F32)
                sg = sg + bias_refs[g][h]
                if g == 0 and n_lead_invalid_fn is not None:
                    col = lax.broadcasted_iota(jnp.int32, sg.shape, 1)
                    sg = jnp.where(col >= n_lead_invalid_fn(pl.program_id(1)), sg, NEG)
                s.append(sg)
            m = functools.reduce(jnp.maximum, [jnp.max(sg, axis=1, keepdims=True) for sg in s])
            p = [jnp.exp2(sg - m) for sg in s]
            l = functools.reduce(jnp.add, [jnp.sum(pg, axis=1, keepdims=True) for pg in p])
            o = functools.reduce(jnp.add, [jnp.dot(pg.astype(BF16), vmat, preferred_element_type=F32)
                                           for pg, vmat in zip(p, vg)])
            outs.append(o / l)
        o_ref[0, :, cols] = jnp.where(upper, outs[1], outs[0]).astype(o_ref.dtype)


def _band_bias(tab, q0, n_q, k0, n_k):
    span = n_q + n_k - 1
    rel = (q0 - k0) + (n_q - 1) - jnp.arange(span)
    u = tab[:, jnp.clip(rel, -REL_CLIP, REL_CLIP) + REL_CLIP].astype(F32) * LOG2E
    n_h = tab.shape[0]
    x = jnp.broadcast_to(jnp.pad(u, ((0, 0), (0, 1)))[:, None, :], (n_h, n_q, span + 1))
    x = x.reshape(n_h, n_q * (span + 1))[:, :n_q * span].reshape(n_h, n_q, span)
    bias = x[:, :, n_q - 1:n_q - 1 + n_k]
    qch = (q0 + jnp.arange(n_q))[:, None] // CHUNK
    kch = (k0 + jnp.arange(n_k))[None, :] // CHUNK
    valid = (kch <= qch) & (kch >= qch - N_PREV_CHUNKS)
    return jnp.where(valid[None], bias, NEG)


def _band_prompt_kernel(qt_ref, *refs, n_seg, tq):
    k_refs = refs[:n_seg]
    vt_refs = refs[n_seg:2 * n_seg]
    bias_ref, o_ref = refs[2 * n_seg], refs[2 * n_seg + 1]
    s_scrs = refs[2 * n_seg + 2:]
    n_keys = n_seg * tq
    lower = lax.broadcasted_iota(jnp.int32, (PAIR, tq), 0) < HEAD_DIM

    def issue_scores(hp):
        cols = slice(hp * PAIR, (hp + 1) * PAIR)
        kmat = jnp.concatenate([r[:, cols] for r in k_refs], axis=0)
        qt = qt_ref[cols, :]
        q2t = jnp.concatenate([jnp.where(lower, qt, jnp.zeros_like(qt)),
                               jnp.where(lower, jnp.zeros_like(qt), qt)], axis=1)
        s_scrs[hp % len(s_scrs)][...] = jnp.dot(kmat, q2t, preferred_element_type=F32)

    def attend(start_mask):
        for hp in range(SCORE_LOOKAHEAD):
            issue_scores(hp)
        for hp in range(N_PAIRS):
            if hp + SCORE_LOOKAHEAD < N_PAIRS:
                issue_scores(hp + SCORE_LOOKAHEAD)
            s_scr = s_scrs[hp % len(s_scrs)]
            ps = []
            for par in range(2):
                s = s_scr[:, par * tq:(par + 1) * tq] + bias_ref[2 * hp + par]
                if start_mask is not None:
                    s = s + start_mask
                ps.append(jnp.exp2(s - jnp.max(s, axis=0, keepdims=True)).astype(BF16))
            vt = jnp.concatenate([r[hp * VT_ROWS:(hp + 1) * VT_ROWS, :] for r in vt_refs], axis=1)
            pv = jnp.dot(vt, jnp.concatenate(ps, axis=1), preferred_element_type=F32)
            o_t = jnp.where(lower, pv[:PAIR, :tq] / pv[PAIR:PAIR + 1, :tq], pv[:PAIR, tq:] / pv[PAIR:PAIR + 1, tq:])
            o_ref[:, hp * PAIR:(hp + 1) * PAIR] = o_t.T.astype(o_ref.dtype)

    n_invalid = ((n_seg - 1) - pl.program_id(1)) * tq

    @pl.when(n_invalid > 0)
    def _():
        attend(jnp.where(lax.broadcasted_iota(jnp.int32, (n_keys, tq), 0) >= n_invalid, 0.0, NEG))

    @pl.when(n_invalid <= 0)
    def _():
        attend(None)


def _band_prompt(qt, k, vt, tab, *, n_batch, seq):
    tq = 2 * CHUNK
    n_seg = (N_PREV_CHUNKS * CHUNK) // tq + 1
    n_qb = seq // tq
    bias_t = jnp.swapaxes(_band_bias(tab, (n_seg - 1) * tq, tq, 0, n_seg * tq), 1, 2)

    def seg_block(b, i, s):
        return b * n_qb + jnp.maximum(i - (n_seg - 1) + s, 0)

    k_specs = [pl.BlockSpec((tq, D_MODEL), lambda b, i, s=s: (seg_block(b, i, s), 0)) for s in range(n_seg)]
    vt_specs = [pl.BlockSpec((N_PAIRS * VT_ROWS, tq), lambda b, i, s=s: (0, seg_block(b, i, s)))
                for s in range(n_seg)]
    return pl.pallas_call(
        functools.partial(_band_prompt_kernel, n_seg=n_seg, tq=tq),
        grid=(n_batch, n_qb),
        in_specs=[pl.BlockSpec((D_MODEL, tq), lambda b, i: (0, b * n_qb + i))] + k_specs + vt_specs
                 + [_const_spec(bias_t.shape)],
        out_specs=pl.BlockSpec((tq, D_MODEL), lambda b, i: (b * n_qb + i, 0)),
        out_shape=jax.ShapeDtypeStruct((n_batch * seq, D_MODEL), BF16),
        scratch_shapes=[pltpu.VMEM((n_seg * tq, 2 * tq), F32)] * (SCORE_LOOKAHEAD + 1),
        compiler_params=_params(("parallel", "parallel")),
    )(qt, *([k] * n_seg), *([vt] * n_seg), bias_t)


def _band_sample(q, k_new, v_new, cache_k, cache_v, tab, past):
    bsz, s_len, _ = q.shape
    a_len = cache_k.shape[1]
    bias_c = _band_bias(tab, past, s_len, past - a_len, a_len)
    bias_n = _band_bias(tab, past, s_len, past, s_len)

    def spec(n):
        return pl.BlockSpec((1, n, D_MODEL), lambda b, i: (b, 0, 0))

    kern = functools.partial(_band_kernel, seg_counts=(1, 1), tq=s_len, n_lead_invalid_fn=None)
    return pl.pallas_call(
        kern,
        grid=(bsz, 1),
        in_specs=[spec(s_len), spec(a_len), spec(s_len), spec(a_len), spec(s_len),
                  _const_spec(bias_c.shape), _const_spec(bias_n.shape)],
        out_specs=spec(s_len),
        out_shape=jax.ShapeDtypeStruct(q.shape, BF16),
        compiler_params=_params(("parallel", "arbitrary")),
    )(q, cache_k, k_new, cache_v, v_new, bias_c, bias_n)


def _dsa_kernel(qt_ref, qit_ref, wit_ref, k_ref, vt_ref, ki_ref, o_ref,
                key_scr, gmax_scr, cut_scr, q2t_scr, qi2t_scr, m_scr, l_scr, acc_scr, *s_scrs,
                tq, tk, n_q_valid, q_off, n_keys, n_sel, slopes, idx_scale):
    i = pl.program_id(1)
    q_pos0 = q_off + i * tq
    last_chunk = (q_pos0 + tq - 1) // CHUNK
    n_adm = jnp.minimum((last_chunk + 1) * CHUNK, n_keys)
    n_tiles = (n_adm + tk - 1) // tk

    q_lane = lax.broadcasted_iota(jnp.int32, (1, tq), 1)
    q_pos = q_pos0 + q_lane
    adm_end = jnp.minimum((q_pos // CHUNK + 1) * CHUNK, n_keys)

    def tile_start(j):
        return pl.multiple_of(j * tk, tk)

    def key_pos(j):
        return j * tk + lax.broadcasted_iota(jnp.int32, (tk, tq), 0)

    wit = wit_ref[...] * idx_scale

    for h2 in range(IDX_HEADS // 2):
        qi2t_scr[h2, :, 0:tq] = qit_ref[(2 * h2) * IDX_DIM:(2 * h2 + 1) * IDX_DIM, :]
        qi2t_scr[h2, :, tq:2 * tq] = qit_ref[(2 * h2 + 1) * IDX_DIM:(2 * h2 + 2) * IDX_DIM, :]

    assert len(s_scrs) >= IDX_HEADS // 2

    def issue_index_logits(j, h2):
        ki = ki_ref[pl.ds(tile_start(jnp.minimum(j, n_tiles - 1)), tk), :]
        lg = jnp.dot(ki, qi2t_scr[h2], preferred_element_type=F32)
        s_scrs[h2][0] = lg[:, :tq]
        s_scrs[h2][1] = lg[:, tq:]

    for h2 in range(IDX_HEADS // 2):
        issue_index_logits(0, h2)

    def score_tile(j, carry):
        acc = jnp.zeros((tk, tq), F32)
        for h2 in range(IDX_HEADS // 2):
            acc = (acc + wit[2 * h2:2 * h2 + 1, :] * jnp.maximum(s_scrs[h2][0], 0.0)
                   + wit[2 * h2 + 1:2 * h2 + 2, :] * jnp.maximum(s_scrs[h2][1], 0.0))
            issue_index_logits(j + 1, h2)
        acc = jnp.where(acc == 0.0, 0.0, acc)
        bits = pltpu.bitcast(acc, jnp.int32)
        key = bits ^ ((bits >> 31) & INT_MAX)
        key = jnp.where(key_pos(j) < adm_end, key, KEY_NEG_INF)
        key_scr[pl.ds(tile_start(j), tk), :] = key
        gmax_scr[...] = jnp.maximum(gmax_scr[...], key.reshape(tk // TOPK_MAX, TOPK_MAX, tq).max(axis=0))
        return carry

    gmax_scr[...] = jnp.full(gmax_scr.shape, INT_MIN, jnp.int32)
    lax.fori_loop(0, n_tiles, score_tile, 0)

    def count(pred):
        def body(j, cnt):
            kt = key_scr[pl.ds(tile_start(j), tk), :]
            hit = pred(kt, j).astype(jnp.int32)
            return cnt + hit.reshape(tk // 8, 8, tq).sum(axis=0)
        cnt = lax.fori_loop(0, n_tiles, body, jnp.zeros((8, tq), jnp.int32))
        return jnp.sum(cnt, axis=0, keepdims=True)

    gmax = gmax_scr[...]
    lo0 = jnp.min(gmax, axis=0, keepdims=True)
    top = jnp.max(gmax, axis=0, keepdims=True)
    hi0 = jnp.where(top == INT_MAX, top, top + 1)
    n_halvings = jnp.max(32 - lax.clz(hi0 - lo0 - 1))

    def thr_step(_, carry):
        lo, hi = carry
        mid = lo + lax.shift_right_logical(hi - lo, 1)
        ok = count(lambda kt, j: kt >= mid) >= n_sel
        return jnp.where(ok, mid, lo), jnp.where(ok, hi, mid)

    thr, _ = lax.fori_loop(0, n_halvings, thr_step, (lo0, hi0))
    cnt_thr = count(lambda kt, j: kt >= thr)

    tie = (cnt_thr > n_sel) & (q_lane < n_q_valid)
    cut_scr[...] = jnp.full((1, tq), INT_MAX, jnp.int32)

    @pl.when(jnp.max(tie.astype(jnp.int32)) > 0)
    def _():
        need = n_sel - count(lambda kt, j: kt > thr)
        n_bits = int(key_scr.shape[0]).bit_length()

        def pos_step(b, cut):
            cand = cut + lax.shift_left(jnp.int32(1), n_bits - 1 - b)
            c = count(lambda kt, j: (kt == thr) & (key_pos(j) < cand))
            return jnp.where(c < need, cand, cut)

        cut = lax.fori_loop(0, n_bits, pos_step, jnp.zeros((1, tq), jnp.int32))
        cut_scr[...] = jnp.where(tie, cut, INT_MAX)

    cut = cut_scr[...]

    lower = lax.broadcasted_iota(jnp.int32, (PAIR, tq), 0) < HEAD_DIM
    for hp in range(N_PAIRS):
        qt = qt_ref[hp * PAIR:(hp + 1) * PAIR, :]
        q2t_scr[hp, :, 0:tq] = jnp.where(lower, qt, jnp.zeros_like(qt))
        q2t_scr[hp, :, tq:2 * tq] = jnp.where(lower, jnp.zeros_like(qt), qt)
    m_scr[...] = jnp.full(m_scr.shape, M_INIT, F32)
    l_scr[...] = jnp.zeros(l_scr.shape, F32)
    acc_scr[...] = jnp.zeros(acc_scr.shape, F32)
    thr_adm = jnp.maximum(thr, KEY_NEG_INF)
    cut_adm = jnp.where(thr > KEY_NEG_INF, cut, -1)

    def attend_tile(j, carry):
        start = tile_start(j)

        def issue_scores(hp):
            kmat = k_ref[pl.ds(start, tk), hp * PAIR:(hp + 1) * PAIR]
            s2 = jnp.dot(kmat, q2t_scr[hp], preferred_element_type=F32)
            s_scrs[hp % len(s_scrs)][0] = s2[:, :tq]
            s_scrs[hp % len(s_scrs)][1] = s2[:, tq:]

        for hp in range(SCORE_LOOKAHEAD):
            issue_scores(hp)
        kt = key_scr[pl.ds(start, tk), :]
        kp = key_pos(j)
        sel = (kt > thr_adm) | ((kt == thr) & (kp <= cut_adm))
        g = jnp.where(sel, -jnp.abs(kp - q_pos).astype(F32), NEG)
        for hp in range(N_PAIRS):
            if hp + SCORE_LOOKAHEAD < N_PAIRS:
                issue_scores(hp + SCORE_LOOKAHEAD)
            s_scr = s_scrs[hp % len(s_scrs)]
            m_prev2, l_prev2 = m_scr[hp], l_scr[hp]
            tops = []
            for par in range(2):
                s_scr[par] = s_scr[par] + slopes[2 * hp + par] * g
                tops.append(jnp.max(s_scr[par], axis=0, keepdims=True))
            m_new2 = jnp.maximum(m_prev2, jnp.concatenate(tops, axis=1))
            alpha2 = jnp.exp2(m_prev2 - m_new2)
            p2 = jnp.concatenate([jnp.exp2(s_scr[0] - m_new2[:, :tq]), jnp.exp2(s_scr[1] - m_new2[:, tq:])],
                                 axis=1).astype(BF16)
            m_scr[hp] = m_new2
            vt = vt_ref[hp * VT_ROWS:(hp + 1) * VT_ROWS, pl.ds(start, tk)]
            pv = jnp.dot(vt, p2, preferred_element_type=F32)
            l_scr[hp] = alpha2 * l_prev2 + pv[PAIR:PAIR + 1]
            acc_old = acc_scr[hp]
            acc_scr[hp] = jnp.where(lower, alpha2[:, :tq] * acc_old + pv[:PAIR, :tq],
                                    alpha2[:, tq:] * acc_old + pv[:PAIR, tq:])
        return carry

    lax.fori_loop(0, n_tiles, attend_tile, 0)

    for hp in range(N_PAIRS):
        l2 = l_scr[hp]
        inv = jnp.where(lower, 1.0 / l2[:, :tq], 1.0 / l2[:, tq:])
        o_ref[:, hp * PAIR:(hp + 1) * PAIR] = (acc_scr[hp] * inv).T.astype(o_ref.dtype)


def _dsa(qt, qit, wit, k, vt, ki, *, n_batch, n_q, l_pad, tq, n_q_valid, q_off, n_keys, n_sel, resident):
    tk = next(t for t in (512, 256) if l_pad % t == 0 and t * tq <= DSA_TILE_ELEMS)
    assert tk % TOPK_MAX == 0 and n_sel <= TOPK_MAX
    n_qb = n_q // tq
    slopes = tuple(float(np.float32(2.0 ** (-8.0 * (h + 1) / N_HEADS))) * LOG2E for h in range(N_HEADS))
    idx_scale = float(IDX_DIM ** -0.5 * IDX_HEADS ** -0.5)
    mode = dict(pipeline_mode=pl.Buffered(1)) if resident else {}

    def qspec(w):
        return pl.BlockSpec((w, tq), lambda b, i: (0, b * n_qb + i))

    kern = functools.partial(_dsa_kernel, tq=tq, tk=tk, n_q_valid=n_q_valid, q_off=q_off, n_keys=n_keys,
                             n_sel=n_sel, slopes=slopes, idx_scale=idx_scale)
    return pl.pallas_call(
        kern,
        grid=(n_batch, n_qb),
        in_specs=[qspec(D_MODEL), qspec(IDX_HEADS * IDX_DIM), qspec(IDX_HEADS),
                  pl.BlockSpec((l_pad, D_MODEL), lambda b, i: (b, 0), **mode),
                  pl.BlockSpec((N_PAIRS * VT_ROWS, l_pad), lambda b, i: (0, b), **mode),
                  pl.BlockSpec((l_pad, IDX_DIM), lambda b, i: (b, 0), **mode)],
        out_specs=pl.BlockSpec((tq, D_MODEL), lambda b, i: (b * n_qb + i, 0)),
        out_shape=jax.ShapeDtypeStruct((n_batch * n_q, D_MODEL), BF16),
        scratch_shapes=[pltpu.VMEM((l_pad, tq), jnp.int32),
                        pltpu.VMEM((TOPK_MAX, tq), jnp.int32),
                        pltpu.VMEM((1, tq), jnp.int32),
                        pltpu.VMEM((N_PAIRS, PAIR, 2 * tq), BF16),
                        pltpu.VMEM((IDX_HEADS // 2, IDX_DIM, 2 * tq), BF16),
                        pltpu.VMEM((N_PAIRS, 1, 2 * tq), F32),
                        pltpu.VMEM((N_PAIRS, 1, 2 * tq), F32),
                        pltpu.VMEM((N_PAIRS, PAIR, tq), F32)]
                       + [pltpu.VMEM((2, tk, tq), F32)] * (SCORE_LOOKAHEAD + 1),
        compiler_params=_params(("parallel", "arbitrary")),
    )(qt, qit, wit, k, vt, ki)


def kernel(x_prompt, x_sample, cache_a_k, cache_a_v, cache_b_k, cache_b_v, cache_b_kidx,
           w_a_qkv, w_a_o, a_rel_bias, w_b_in, w_b_o, w_ff1, w_ff2, ln1_g, ln1_b, ln2_g, ln2_b):
    bsz, seq, _ = x_prompt.shape
    dbsz, dseq, _ = x_sample.shape
    depth = w_ff1.shape[0]
    past = cache_b_k.shape[2]
    alpha = float((2.0 * depth) ** 0.25)
    n_p = bsz * seq
    n_s = dbsz * dseq
    q_scale = float(HEAD_DIM ** -0.5) * LOG2E

    x = jnp.concatenate([x_prompt.reshape(n_p, D_MODEL), x_sample.reshape(n_s, D_MODEL)], axis=0)

    def split(a, width):
        return a[:n_p].reshape(bsz, seq, width), a[n_p:].reshape(dbsz, dseq, width)

    def heads(a):
        return a.reshape(a.shape[:-1] + (N_HEADS, HEAD_DIM))

    ak_p, av_p, ak_s, av_s = [], [], [], []
    bk_p, bv_p, bi_p, bk_s, bv_s, bi_s = [], [], [], [], [], []
    for i in range(depth):
        j = i // 2
        if i % 2 == 0:
            groups = [(0, D_MODEL, [(0, q_scale, "rows_tail"), (1, q_scale, "cols")]),
                      (D_MODEL, D_MODEL, [(2, 1.0, "rows_head"), (3, 1.0, "rows_tail"), (4, 1.0, "rows")]),
                      (2 * D_MODEL, D_MODEL, [(5, 1.0, "rows_head"), (6, 1.0, "rows_tail"), (7, 1.0, "rows_tail"),
                                              (8, 1.0, "pair_cols")])]
            defs = [(D_MODEL, BF16, "rows_tail"), (D_MODEL, BF16, "cols"),
                    (D_MODEL, F32, "rows_head"), (D_MODEL, F32, "rows_tail"), (D_MODEL, BF16, "rows"),
                    (D_MODEL, F32, "rows_head"), (D_MODEL, F32, "rows_tail"), (D_MODEL, BF16, "rows_tail"),
                    (N_PAIRS * VT_ROWS, BF16, "pair_cols")]
            q_s, qt, k32_p, k32_s, k16, v32_p, v32_s, v_s, vt = _proj(x, w_a_qkv[j], groups, defs, n_p)
            o_p = _band_prompt(qt, k16, vt, a_rel_bias[j], n_batch=bsz, seq=seq)
            q_s = q_s.reshape(dbsz, dseq, D_MODEL)
            k_s = k16[n_p:].reshape(dbsz, dseq, D_MODEL)
            v_s = v_s.reshape(dbsz, dseq, D_MODEL)
            a_len = cache_a_k.shape[2]
            ck = cache_a_k[j].reshape(dbsz, a_len, D_MODEL).astype(BF16)
            cv = cache_a_v[j].reshape(dbsz, a_len, D_MODEL).astype(BF16)
            o_s = _band_sample(q_s, k_s, v_s, ck, cv, a_rel_bias[j], past).reshape(n_s, D_MODEL)
            n_keep = min(N_PREV_CHUNKS * CHUNK, seq)
            ak_p.append(heads(k32_p.reshape(bsz, seq, D_MODEL)[:, seq - n_keep:]))
            av_p.append(heads(v32_p.reshape(bsz, seq, D_MODEL)[:, seq - n_keep:]))
            ak_s.append(heads(k32_s.reshape(dbsz, dseq, D_MODEL)))
            av_s.append(heads(v32_s.reshape(dbsz, dseq, D_MODEL)))
            w_o = w_a_o[j]
        else:
            w_in = w_b_in[j]
            w_cat = jnp.concatenate(
                [w_in[:, :3 * D_MODEL], jnp.pad(w_in[:, 3 * D_MODEL:], ((0, 0), (0, IDX_PAD - IDX_WIDTH)))],
                axis=1)
            n_qi = IDX_HEADS * IDX_DIM
            groups = [(0, D_MODEL, [(0, q_scale, "cols")]),
                      (D_MODEL, D_MODEL, [(1, 1.0, "rows_head"), (2, 1.0, "rows_tail"), (3, 1.0, "rows")]),
                      (2 * D_MODEL, D_MODEL, [(4, 1.0, "rows_head"), (5, 1.0, "rows_tail"), (6, 1.0, "pair_cols")]),
                      (3 * D_MODEL, n_qi, [(7, 1.0, "cols")]),
                      (3 * D_MODEL + n_qi, IDX_PAD - n_qi, [(8, 1.0, "rows"), (9, 1.0, "cols")])]
            defs = [(D_MODEL, BF16, "cols"),
                    (D_MODEL, F32, "rows_head"), (D_MODEL, F32, "rows_tail"), (D_MODEL, BF16, "rows"),
                    (D_MODEL, F32, "rows_head"), (D_MODEL, F32, "rows_tail"), (N_PAIRS * VT_ROWS, BF16, "pair_cols"),
                    (n_qi, BF16, "cols"), (IDX_PAD - n_qi, F32, "rows"), (IDX_PAD - n_qi, F32, "cols")]
            qt, k32_p, k32_s, k16, v32_p, v32_s, vt, qit, kiw, kiwt = _proj(x, w_cat, groups, defs, n_p)
            wit = kiwt[IDX_DIM:IDX_DIM + IDX_HEADS]
            ki32 = kiw[:, :IDX_DIM]
            ki16 = ki32.astype(BF16)

            tq_p = _row_tile(seq, DSA_PROMPT_QUERIES)
            o_p = _dsa(qt, qit, wit, k16, vt, ki16, n_batch=bsz, n_q=seq, l_pad=seq, tq=tq_p, n_q_valid=tq_p,
                       q_off=0, n_keys=seq, n_sel=min(TOPK_MAX, seq // 4), resident=True)
            tq = 2 * CHUNK

            n_keys = past + dseq
            l_pad = -(-n_keys // TOPK_MAX) * TOPK_MAX
            tq_s = -(-dseq // tq) * tq

            def pad_q(a):
                w = a.shape[0]
                a = jnp.pad(a[:, n_p:].reshape(w, dbsz, dseq), ((0, 0), (0, 0), (0, tq_s - dseq)))
                return a.reshape(w, dbsz * tq_s)

            def keys_rows(cache, new):
                w = new.shape[-1]
                a = jnp.concatenate([cache.astype(BF16), new.reshape(dbsz, dseq, w)], axis=1)
                return jnp.pad(a, ((0, 0), (0, l_pad - n_keys), (0, 0))).reshape(dbsz * l_pad, w)

            k_all = keys_rows(cache_b_k[j].reshape(dbsz, past, D_MODEL), k16[n_p:])
            ki_all = keys_rows(cache_b_kidx[j], ki16[n_p:])
            cvt = jnp.transpose(cache_b_v[j].reshape(dbsz, past, N_PAIRS, PAIR).astype(BF16), (2, 3, 0, 1))
            cvt = jnp.concatenate([cvt, jnp.ones((N_PAIRS, ONES_ROWS, dbsz, past), BF16)], axis=1)
            vt_all = jnp.concatenate([cvt.reshape(N_PAIRS * VT_ROWS, dbsz, past),
                                      vt[:, n_p:].reshape(N_PAIRS * VT_ROWS, dbsz, dseq)], axis=2)
            vt_all = jnp.pad(vt_all, ((0, 0), (0, 0), (0, l_pad - n_keys))).reshape(N_PAIRS * VT_ROWS, dbsz * l_pad)
            o_s = _dsa(pad_q(qt), pad_q(qit), pad_q(wit), k_all, vt_all, ki_all, n_batch=dbsz, n_q=tq_s,
                       l_pad=l_pad, tq=tq, n_q_valid=dseq, q_off=past, n_keys=n_keys,
                       n_sel=min(TOPK_MAX, n_keys // 4), resident=False)
            o_s = o_s.reshape(dbsz, tq_s, D_MODEL)[:, :dseq].reshape(n_s, D_MODEL)

            ki_p, ki_s = split(ki32, IDX_DIM)
            bk_p.append(heads(k32_p.reshape(bsz, seq, D_MODEL))); bv_p.append(heads(v32_p.reshape(bsz, seq, D_MODEL)))
            bk_s.append(heads(k32_s.reshape(dbsz, dseq, D_MODEL))); bv_s.append(heads(v32_s.reshape(dbsz, dseq, D_MODEL)))
            bi_p.append(ki_p); bi_s.append(ki_s)
            w_o = w_b_o[j]

        o = jnp.concatenate([o_p, o_s], axis=0)
        x = _oproj_ln(o, w_o, x, ln1_g[i], ln1_b[i], alpha)
        x = _ffn_ln(x, w_ff1[i], w_ff2[i], ln2_g[i], ln2_b[i], alpha)

    y_p, y_s = split(x, D_MODEL)
    return (y_p, y_s,
            jnp.stack(ak_p), jnp.stack(av_p), jnp.stack(bk_p), jnp.stack(bv_p), jnp.stack(bi_p),
            jnp.stack(ak_s), jnp.stack(av_s), jnp.stack(bk_s), jnp.stack(bv_s), jnp.stack(bi_s))
```

```python
import functools
import math

import numpy as np
import jax
import jax.numpy as jnp
from jax import lax
from jax.experimental import pallas as pl
from jax.experimental.pallas import tpu as pltpu

D_MODEL = 1024
N_HEADS = 16
HEAD_DIM = 64
PAIR = 2 * HEAD_DIM
N_PAIRS = N_HEADS // 2
CHUNK = 64
N_PREV_CHUNKS = 8
REL_CLIP = 128
IDX_HEADS = 8
IDX_DIM = 64
IDX_WIDTH = IDX_HEADS * IDX_DIM + IDX_DIM + IDX_HEADS
IDX_PAD = 640
TOPK_MAX = 256
LN_EPS = 1e-5
LOG2E = math.log2(math.e)

NEG = -1e30
M_INIT = -3e38
SCORE_LOOKAHEAD = 3
DSA_PROMPT_QUERIES = 128
DSA_TILE_ELEMS = 512 * 128
ONES_ROWS = 16
VT_ROWS = PAIR + ONES_ROWS
KEY_NEG_INF = -2139095041
INT_MIN = -2147483648
INT_MAX = 2147483647
V7X_VMEM_LIMIT = 60 * 1024 * 1024

F32 = jnp.float32
BF16 = jnp.bfloat16


def _params(sem, flags=None):
    return pltpu.CompilerParams(dimension_semantics=sem, vmem_limit_bytes=V7X_VMEM_LIMIT, flags=flags)


def _row_tile(m, cap):
    t = cap
    while m % t:
        t //= 2
    return t


def _const_spec(shape):
    nd = len(shape)
    return pl.BlockSpec(shape, lambda *_: (0,) * nd, pipeline_mode=pl.Buffered(1))


def _proj_kernel(x_ref, w_ref, *out_refs, groups, n_head_blocks):
    i = pl.program_id(0)
    xb = x_ref[...].astype(BF16)
    for col, width, outs in groups:
        r = jnp.dot(xb, w_ref[:, col:col + width].astype(BF16), preferred_element_type=F32)
        for idx, scale, layout in outs:
            o = out_refs[idx]
            v = r * scale if scale != 1.0 else r
            if layout == "rows":
                o[...] = v.astype(o.dtype)
            elif layout in ("rows_head", "rows_tail"):
                @pl.when((i < n_head_blocks) if layout == "rows_head" else (i >= n_head_blocks))
                def _(o=o, v=v):
                    o[...] = v.astype(o.dtype)
            elif layout == "cols":
                o[...] = v.T.astype(o.dtype)
            else:
                vt = v.T.astype(o.dtype)
                for hp in range(N_PAIRS):
                    o[hp * VT_ROWS:hp * VT_ROWS + PAIR, :] = vt[hp * PAIR:(hp + 1) * PAIR]
                    o[hp * VT_ROWS + PAIR:(hp + 1) * VT_ROWS, :] = jnp.ones((ONES_ROWS, vt.shape[1]), o.dtype)


def _proj(x, w, groups, out_defs, n_head):
    m = x.shape[0]
    tm = _row_tile(math.gcd(m, n_head), 512)
    nhb = n_head // tm
    shapes = {"rows": lambda wd: (m, wd), "rows_head": lambda wd: (n_head, wd), "rows_tail": lambda wd: (m - n_head, wd),
              "cols": lambda wd: (wd, m), "pair_cols": lambda wd: (wd, m)}
    specs = {"rows": lambda wd: pl.BlockSpec((tm, wd), lambda i: (i, 0)),
             "rows_head": lambda wd: pl.BlockSpec((tm, wd), lambda i: (jnp.minimum(i, nhb - 1), 0)),
             "rows_tail": lambda wd: pl.BlockSpec((tm, wd), lambda i: (jnp.maximum(i - nhb, 0), 0)),
             "cols": lambda wd: pl.BlockSpec((wd, tm), lambda i: (0, i)),
             "pair_cols": lambda wd: pl.BlockSpec((wd, tm), lambda i: (0, i))}
    out_shape = [jax.ShapeDtypeStruct(shapes[lay](wd), dt) for wd, dt, lay in out_defs]
    out_specs = [specs[lay](wd) for wd, _, lay in out_defs]
    return pl.pallas_call(
        functools.partial(_proj_kernel, groups=groups, n_head_blocks=nhb),
        grid=(m // tm,),
        in_specs=[pl.BlockSpec((tm, D_MODEL), lambda i: (i, 0)), _const_spec(w.shape)],
        out_specs=out_specs,
        out_shape=out_shape,
        compiler_params=_params(("arbitrary",)),
    )(x, w)


def _deepnorm_ln(x, r, g, b, alpha):
    y = alpha * x + r
    mu = jnp.mean(y, axis=-1, keepdims=True)
    yc = y - mu
    var = jnp.mean(yc * yc, axis=-1, keepdims=True)
    return yc * lax.rsqrt(var + LN_EPS) * g + b


def _oproj_ln_kernel(o_head_ref, o_tail_ref, w_ref, x_ref, g_ref, b_ref, out_ref, *, alpha, n_head_blocks):
    def run(o_ref):
        r = jnp.dot(o_ref[...], w_ref[...].astype(BF16), preferred_element_type=F32)
        out_ref[...] = _deepnorm_ln(x_ref[...], r, g_ref[...], b_ref[...], alpha)

    pl.when(pl.program_id(0) < n_head_blocks)(functools.partial(run, o_head_ref))
    pl.when(pl.program_id(0) >= n_head_blocks)(functools.partial(run, o_tail_ref))


def _oproj_ln(o_head, o_tail, w, x, g, b, alpha):
    m = x.shape[0]
    n_head = o_head.shape[0]
    tm = _row_tile(math.gcd(m, n_head), 1024)
    nhb = n_head // tm
    row = pl.BlockSpec((tm, D_MODEL), lambda i: (i, 0))
    return pl.pallas_call(
        functools.partial(_oproj_ln_kernel, alpha=alpha, n_head_blocks=nhb),
        grid=(m // tm,),
        in_specs=[pl.BlockSpec((tm, D_MODEL), lambda i: (jnp.minimum(i, nhb - 1), 0)),
                  pl.BlockSpec((tm, D_MODEL), lambda i: (jnp.maximum(i - nhb, 0), 0)),
                  _const_spec(w.shape), row, _const_spec((1, D_MODEL)), _const_spec((1, D_MODEL))],
        out_specs=row,
        out_shape=jax.ShapeDtypeStruct((m, D_MODEL), F32),
        compiler_params=_params(("arbitrary",)),
    )(o_head, o_tail, w, x, g.reshape(1, D_MODEL), b.reshape(1, D_MODEL))


def _ffn_ln_kernel(x_ref, w1_ref, w2_ref, g_ref, b_ref, out_ref, acc_ref, *, alpha):
    f = pl.program_id(1)

    @pl.when(f == 0)
    def _():
        acc_ref[...] = jnp.zeros_like(acc_ref)

    h = jnp.dot(x_ref[...].astype(BF16), w1_ref[...].astype(BF16), preferred_element_type=F32)
    h = jnp.maximum(h, 0.0)
    acc_ref[...] += jnp.dot((h * h).astype(BF16), w2_ref[...].astype(BF16), preferred_element_type=F32)

    @pl.when(f == pl.num_programs(1) - 1)
    def _():
        out_ref[...] = _deepnorm_ln(x_ref[...], acc_ref[...], g_ref[...], b_ref[...], alpha)


def _ffn_ln(x, w1, w2, g, b, alpha):
    m = x.shape[0]
    d_ff = w1.shape[1]
    tm = _row_tile(m, 1024)
    tf = _row_tile(d_ff, 1024)
    row = pl.BlockSpec((tm, D_MODEL), lambda i, f: (i, 0))
    return pl.pallas_call(
        functools.partial(_ffn_ln_kernel, alpha=alpha),
        grid=(m // tm, d_ff // tf),
        in_specs=[row,
                  pl.BlockSpec((D_MODEL, tf), lambda i, f: (0, f)),
                  pl.BlockSpec((tf, D_MODEL), lambda i, f: (f, 0)),
                  _const_spec((1, D_MODEL)), _const_spec((1, D_MODEL))],
        out_specs=row,
        out_shape=jax.ShapeDtypeStruct((m, D_MODEL), F32),
        scratch_shapes=[pltpu.VMEM((tm, D_MODEL), F32)],
        compiler_params=_params(("parallel", "arbitrary")),
    )(x, w1, w2, g.reshape(1, D_MODEL), b.reshape(1, D_MODEL))


def _band_kernel(*refs, seg_counts, n_lead_invalid_fn, tq):
    n_seg = sum(seg_counts)
    q_ref = refs[0]
    k_refs = refs[1:1 + n_seg]
    v_refs = refs[1 + n_seg:1 + 2 * n_seg]
    bias_refs = refs[1 + 2 * n_seg:1 + 2 * n_seg + len(seg_counts)]
    o_ref = refs[-1]

    lane = lax.broadcasted_iota(jnp.int32, (tq, PAIR), 1)
    upper = lane >= HEAD_DIM
    for hp in range(N_PAIRS):
        cols = slice(hp * PAIR, (hp + 1) * PAIR)
        qp = q_ref[0, :, cols]
        kg, vg = [], []
        s0 = 0
        for cnt in seg_counts:
            ks = [k_refs[s0 + t][0, :, cols] for t in range(cnt)]
            vs = [v_refs[s0 + t][0, :, cols] for t in range(cnt)]
            kg.append(ks[0] if cnt == 1 else jnp.concatenate(ks, axis=0))
            vg.append(vs[0] if cnt == 1 else jnp.concatenate(vs, axis=0))
            s0 += cnt
        outs = []
        for par in range(2):
            h = 2 * hp + par
            qm = jnp.where(upper == (par == 1), qp, jnp.zeros_like(qp))
            s = []
            for g, kmat in enumerate(kg):
                sg = lax.dot_general(qm, kmat, (((1,), (1,)), ((), ())), preferred_element_type=F32)
                sg = sg + bias_refs[g][h]
                if g == 0 and n_lead_invalid_fn is not None:
                    col = lax.broadcasted_iota(jnp.int32, sg.shape, 1)
                    sg = jnp.where(col >= n_lead_invalid_fn(pl.program_id(1)), sg, NEG)
                s.append(sg)
            m = functools.reduce(jnp.maximum, [jnp.max(sg, axis=1, keepdims=True) for sg in s])
            p = [jnp.exp2(sg - m) for sg in s]
            l = functools.reduce(jnp.add, [jnp.sum(pg, axis=1, keepdims=True) for pg in p])
            o = functools.reduce(jnp.add, [jnp.dot(pg.astype(BF16), vmat, preferred_element_type=F32)
                                           for pg, vmat in zip(p, vg)])
            outs.append(o / l)
        o_ref[0, :, cols] = jnp.where(upper, outs[1], outs[0]).astype(o_ref.dtype)


def _band_bias(tab, q0, n_q, k0, n_k):
    span = n_q + n_k - 1
    rel = (q0 - k0) + (n_q - 1) - jnp.arange(span)
    u = tab[:, jnp.clip(rel, -REL_CLIP, REL_CLIP) + REL_CLIP].astype(F32) * LOG2E
    n_h = tab.shape[0]
    x = jnp.broadcast_to(jnp.pad(u, ((0, 0), (0, 1)))[:, None, :], (n_h, n_q, span + 1))
    x = x.reshape(n_h, n_q * (span + 1))[:, :n_q * span].reshape(n_h, n_q, span)
    bias = x[:, :, n_q - 1:n_q - 1 + n_k]
    qch = (q0 + jnp.arange(n_q))[:, None] // CHUNK
    kch = (k0 + jnp.arange(n_k))[None, :] // CHUNK
    valid = (kch <= qch) & (kch >= qch - N_PREV_CHUNKS)
    return jnp.where(valid[None], bias, NEG)


def _band_prompt_kernel(qt_ref, *refs, n_seg, tq):
    k_refs = refs[:n_seg]
    vt_refs = refs[n_seg:2 * n_seg]
    bias_ref, o_ref = refs[2 * n_seg], refs[2 * n_seg + 1]
    s_scrs = refs[2 * n_seg + 2:]
    n_keys = n_seg * tq
    lower = lax.broadcasted_iota(jnp.int32, (PAIR, tq), 0) < HEAD_DIM

    def issue_scores(hp):
        cols = slice(hp * PAIR, (hp + 1) * PAIR)
        kmat = jnp.concatenate([r[:, cols] for r in k_refs], axis=0)
        qt = qt_ref[cols, :]
        q2t = jnp.concatenate([jnp.where(lower, qt, jnp.zeros_like(qt)),
                               jnp.where(lower, jnp.zeros_like(qt), qt)], axis=1)
        s_scrs[hp % len(s_scrs)][...] = jnp.dot(kmat, q2t, preferred_element_type=F32)

    def attend(start_mask):
        for hp in range(SCORE_LOOKAHEAD):
            issue_scores(hp)
        for hp in range(N_PAIRS):
            if hp + SCORE_LOOKAHEAD < N_PAIRS:
                issue_scores(hp + SCORE_LOOKAHEAD)
            s_scr = s_scrs[hp % len(s_scrs)]
            ps = []
            for par in range(2):
                s = s_scr[:, par * tq:(par + 1) * tq] + bias_ref[2 * hp + par]
                if start_mask is not None:
                    s = s + start_mask
                ps.append(jnp.exp2(s - jnp.max(s, axis=0, keepdims=True)).astype(BF16))
            vt = jnp.concatenate([r[hp * VT_ROWS:(hp + 1) * VT_ROWS, :] for r in vt_refs], axis=1)
            pv = jnp.dot(vt, jnp.concatenate(ps, axis=1), preferred_element_type=F32)
            o_t = jnp.where(lower, pv[:PAIR, :tq] / pv[PAIR:PAIR + 1, :tq], pv[:PAIR, tq:] / pv[PAIR:PAIR + 1, tq:])
            o_ref[:, hp * PAIR:(hp + 1) * PAIR] = o_t.T.astype(o_ref.dtype)

    n_invalid = ((n_seg - 1) - pl.program_id(1)) * tq

    @pl.when(n_invalid > 0)
    def _():
        attend(jnp.where(lax.broadcasted_iota(jnp.int32, (n_keys, tq), 0) >= n_invalid, 0.0, NEG))

    @pl.when(n_invalid <= 0)
    def _():
        attend(None)


def _band_prompt(qt, k, vt, tab, *, n_batch, seq):
    tq = 2 * CHUNK
    n_seg = (N_PREV_CHUNKS * CHUNK) // tq + 1
    n_qb = seq // tq
    bias_t = jnp.swapaxes(_band_bias(tab, (n_seg - 1) * tq, tq, 0, n_seg * tq), 1, 2)

    def seg_block(b, i, s):
        return b * n_qb + jnp.maximum(i - (n_seg - 1) + s, 0)

    k_specs = [pl.BlockSpec((tq, D_MODEL), lambda b, i, s=s: (seg_block(b, i, s), 0)) for s in range(n_seg)]
    vt_specs = [pl.BlockSpec((N_PAIRS * VT_ROWS, tq), lambda b, i, s=s: (0, seg_block(b, i, s)))
                for s in range(n_seg)]
    return pl.pallas_call(
        functools.partial(_band_prompt_kernel, n_seg=n_seg, tq=tq),
        grid=(n_batch, n_qb),
        in_specs=[pl.BlockSpec((D_MODEL, tq), lambda b, i: (0, b * n_qb + i))] + k_specs + vt_specs
                 + [_const_spec(bias_t.shape)],
        out_specs=pl.BlockSpec((tq, D_MODEL), lambda b, i: (b * n_qb + i, 0)),
        out_shape=jax.ShapeDtypeStruct((n_batch * seq, D_MODEL), BF16),
        scratch_shapes=[pltpu.VMEM((n_seg * tq, 2 * tq), F32)] * (SCORE_LOOKAHEAD + 1),
        compiler_params=_params(("parallel", "parallel")),
    )(qt, *([k] * n_seg), *([vt] * n_seg), bias_t)


def _band_sample(q, k_new, v_new, cache_k, cache_v, tab, past):
    bsz, s_len, _ = q.shape
    a_len = cache_k.shape[1]
    bias_c = _band_bias(tab, past, s_len, past - a_len, a_len)
    bias_n = _band_bias(tab, past, s_len, past, s_len)

    def spec(n):
        return pl.BlockSpec((1, n, D_MODEL), lambda b, i: (b, 0, 0))

    kern = functools.partial(_band_kernel, seg_counts=(1, 1), tq=s_len, n_lead_invalid_fn=None)
    return pl.pallas_call(
        kern,
        grid=(bsz, 1),
        in_specs=[spec(s_len), spec(a_len), spec(s_len), spec(a_len), spec(s_len),
                  _const_spec(bias_c.shape), _const_spec(bias_n.shape)],
        out_specs=spec(s_len),
        out_shape=jax.ShapeDtypeStruct(q.shape, BF16),
        compiler_params=_params(("parallel", "arbitrary")),
    )(q, cache_k, k_new, cache_v, v_new, bias_c, bias_n)


def _dsa_kernel(qt_ref, qit_ref, wit_ref, k_ref, vt_ref, ki_ref, o_ref,
                key_scr, gmax_scr, cut_scr, q2t_scr, qi2t_scr, m_scr, l_scr, acc_scr, *s_scrs,
                tq, tk, n_q_valid, q_off, n_keys, n_sel, slopes, idx_scale):
    i = pl.program_id(1)
    q_pos0 = q_off + i * tq
    last_chunk = (q_pos0 + tq - 1) // CHUNK
    n_adm = jnp.minimum((last_chunk + 1) * CHUNK, n_keys)
    n_tiles = (n_adm + tk - 1) // tk

    q_lane = lax.broadcasted_iota(jnp.int32, (1, tq), 1)
    q_pos = q_pos0 + q_lane
    adm_end = jnp.minimum((q_pos // CHUNK + 1) * CHUNK, n_keys)

    def tile_start(j):
        return pl.multiple_of(j * tk, tk)

    def key_pos(j):
        return j * tk + lax.broadcasted_iota(jnp.int32, (tk, tq), 0)

    wit = wit_ref[...] * idx_scale

    for h2 in range(IDX_HEADS // 2):
        qi2t_scr[h2, :, 0:tq] = qit_ref[(2 * h2) * IDX_DIM:(2 * h2 + 1) * IDX_DIM, :]
        qi2t_scr[h2, :, tq:2 * tq] = qit_ref[(2 * h2 + 1) * IDX_DIM:(2 * h2 + 2) * IDX_DIM, :]

    assert len(s_scrs) >= IDX_HEADS // 2

    def issue_index_logits(j, h2):
        ki = ki_ref[pl.ds(tile_start(jnp.minimum(j, n_tiles - 1)), tk), :]
        lg = jnp.dot(ki, qi2t_scr[h2], preferred_element_type=F32)
        s_scrs[h2][0] = lg[:, :tq]
        s_scrs[h2][1] = lg[:, tq:]

    for h2 in range(IDX_HEADS // 2):
        issue_index_logits(0, h2)

    def score_tile(j, carry):
        acc = jnp.zeros((tk, tq), F32)
        for h2 in range(IDX_HEADS // 2):
            acc = (acc + wit[2 * h2:2 * h2 + 1, :] * jnp.maximum(s_scrs[h2][0], 0.0)
                   + wit[2 * h2 + 1:2 * h2 + 2, :] * jnp.maximum(s_scrs[h2][1], 0.0))
            issue_index_logits(j + 1, h2)
        acc = jnp.where(acc == 0.0, 0.0, acc)
        bits = pltpu.bitcast(acc, jnp.int32)
        key = bits ^ ((bits >> 31) & INT_MAX)
        key = jnp.where(key_pos(j) < adm_end, key, KEY_NEG_INF)
        key_scr[pl.ds(tile_start(j), tk), :] = key
        gmax_scr[...] = jnp.maximum(gmax_scr[...], key.reshape(tk // TOPK_MAX, TOPK_MAX, tq).max(axis=0))
        return carry

    gmax_scr[...] = jnp.full(gmax_scr.shape, INT_MIN, jnp.int32)
    lax.fori_loop(0, n_tiles, score_tile, 0)

    def count(pred):
        def body(j, cnt):
            kt = key_scr[pl.ds(tile_start(j), tk), :]
            hit = pred(kt, j).astype(jnp.int32)
            return cnt + hit.reshape(tk // 8, 8, tq).sum(axis=0)
        cnt = lax.fori_loop(0, n_tiles, body, jnp.zeros((8, tq), jnp.int32))
        return jnp.sum(cnt, axis=0, keepdims=True)

    gmax = gmax_scr[...]
    lo0 = jnp.min(gmax, axis=0, keepdims=True)
    top = jnp.max(gmax, axis=0, keepdims=True)
    hi0 = jnp.where(top == INT_MAX, top, top + 1)
    n_halvings = jnp.max(32 - lax.clz(hi0 - lo0 - 1))

    def thr_step(_, carry):
        lo, hi = carry
        mid = lo + lax.shift_right_logical(hi - lo, 1)
        ok = count(lambda kt, j: kt >= mid) >= n_sel
        return jnp.where(ok, mid, lo), jnp.where(ok, hi, mid)

    thr, _ = lax.fori_loop(0, n_halvings, thr_step, (lo0, hi0))
    cnt_thr = count(lambda kt, j: kt >= thr)

    tie = (cnt_thr > n_sel) & (q_lane < n_q_valid)
    cut_scr[...] = jnp.full((1, tq), INT_MAX, jnp.int32)

    @pl.when(jnp.max(tie.astype(jnp.int32)) > 0)
    def _():
        need = n_sel - count(lambda kt, j: kt > thr)
        n_bits = int(key_scr.shape[0]).bit_length()

        def pos_step(b, cut):
            cand = cut + lax.shift_left(jnp.int32(1), n_bits - 1 - b)
            c = count(lambda kt, j: (kt == thr) & (key_pos(j) < cand))
            return jnp.where(c < need, cand, cut)

        cut = lax.fori_loop(0, n_bits, pos_step, jnp.zeros((1, tq), jnp.int32))
        cut_scr[...] = jnp.where(tie, cut, INT_MAX)

    cut = cut_scr[...]

    lower = lax.broadcasted_iota(jnp.int32, (PAIR, tq), 0) < HEAD_DIM
    for hp in range(N_PAIRS):
        qt = qt_ref[hp * PAIR:(hp + 1) * PAIR, :]
        q2t_scr[hp, :, 0:tq] = jnp.where(lower, qt, jnp.zeros_like(qt))
        q2t_scr[hp, :, tq:2 * tq] = jnp.where(lower, jnp.zeros_like(qt), qt)
    m_scr[...] = jnp.full(m_scr.shape, M_INIT, F32)
    l_scr[...] = jnp.zeros(l_scr.shape, F32)
    acc_scr[...] = jnp.zeros(acc_scr.shape, F32)
    thr_adm = jnp.maximum(thr, KEY_NEG_INF)
    cut_adm = jnp.where(thr > KEY_NEG_INF, cut, -1)

    def attend_tile(j, carry):
        start = tile_start(j)

        def issue_scores(hp):
            kmat = k_ref[pl.ds(start, tk), hp * PAIR:(hp + 1) * PAIR]
            s2 = jnp.dot(kmat, q2t_scr[hp], preferred_element_type=F32)
            s_scrs[hp % len(s_scrs)][0] = s2[:, :tq]
            s_scrs[hp % len(s_scrs)][1] = s2[:, tq:]

        for hp in range(SCORE_LOOKAHEAD):
            issue_scores(hp)
        kt = key_scr[pl.ds(start, tk), :]
        kp = key_pos(j)
        sel = (kt > thr_adm) | ((kt == thr) & (kp <= cut_adm))
        g = jnp.where(sel, -jnp.abs(kp - q_pos).astype(F32), NEG)
        for hp in range(N_PAIRS):
            if hp + SCORE_LOOKAHEAD < N_PAIRS:
                issue_scores(hp + SCORE_LOOKAHEAD)
            s_scr = s_scrs[hp % len(s_scrs)]
            m_prev2, l_prev2 = m_scr[hp], l_scr[hp]
            tops = []
            for par in range(2):
                s_scr[par] = s_scr[par] + slopes[2 * hp + par] * g
                tops.append(jnp.max(s_scr[par], axis=0, keepdims=True))
            m_new2 = jnp.maximum(m_prev2, jnp.concatenate(tops, axis=1))
            alpha2 = jnp.exp2(m_prev2 - m_new2)
            p2 = jnp.concatenate([jnp.exp2(s_scr[0] - m_new2[:, :tq]), jnp.exp2(s_scr[1] - m_new2[:, tq:])],
                                 axis=1).astype(BF16)
            m_scr[hp] = m_new2
            vt = vt_ref[hp * VT_ROWS:(hp + 1) * VT_ROWS, pl.ds(start, tk)]
            pv = jnp.dot(vt, p2, preferred_element_type=F32)
            l_scr[hp] = alpha2 * l_prev2 + pv[PAIR:PAIR + 1]
            acc_old = acc_scr[hp]
            acc_scr[hp] = jnp.where(lower, alpha2[:, :tq] * acc_old + pv[:PAIR, :tq],
                                    alpha2[:, tq:] * acc_old + pv[:PAIR, tq:])
        return carry

    lax.fori_loop(0, n_tiles, attend_tile, 0)

    for hp in range(N_PAIRS):
        l2 = l_scr[hp]
        inv = jnp.where(lower, 1.0 / l2[:, :tq], 1.0 / l2[:, tq:])
        o_ref[:, hp * PAIR:(hp + 1) * PAIR] = (acc_scr[hp] * inv).T.astype(o_ref.dtype)


def _dsa(qt, qit, wit, k, vt, ki, *, n_batch, n_q, l_pad, tq, n_q_valid, q_off, n_keys, n_sel, resident):
    tk = next(t for t in (512, 256) if l_pad % t == 0 and t * tq <= DSA_TILE_ELEMS)
    assert tk % TOPK_MAX == 0 and n_sel <= TOPK_MAX
    n_qb = n_q // tq
    slopes = tuple(float(np.float32(2.0 ** (-8.0 * (h + 1) / N_HEADS))) * LOG2E for h in range(N_HEADS))
    idx_scale = float(IDX_DIM ** -0.5 * IDX_HEADS ** -0.5)
    mode = dict(pipeline_mode=pl.Buffered(1)) if resident else {}

    def qspec(w):
        return pl.BlockSpec((w, tq), lambda b, i: (0, b * n_qb + i))

    kern = functools.partial(_dsa_kernel, tq=tq, tk=tk, n_q_valid=n_q_valid, q_off=q_off, n_keys=n_keys,
                             n_sel=n_sel, slopes=slopes, idx_scale=idx_scale)
    return pl.pallas_call(
        kern,
        grid=(n_batch, n_qb),
        in_specs=[qspec(D_MODEL), qspec(IDX_HEADS * IDX_DIM), qspec(IDX_HEADS),
                  pl.BlockSpec((l_pad, D_MODEL), lambda b, i: (b, 0), **mode),
                  pl.BlockSpec((N_PAIRS * VT_ROWS, l_pad), lambda b, i: (0, b), **mode),
                  pl.BlockSpec((l_pad, IDX_DIM), lambda b, i: (b, 0), **mode)],
        out_specs=pl.BlockSpec((tq, D_MODEL), lambda b, i: (b * n_qb + i, 0)),
        out_shape=jax.ShapeDtypeStruct((n_batch * n_q, D_MODEL), BF16),
        scratch_shapes=[pltpu.VMEM((l_pad, tq), jnp.int32),
                        pltpu.VMEM((TOPK_MAX, tq), jnp.int32),
                        pltpu.VMEM((1, tq), jnp.int32),
                        pltpu.VMEM((N_PAIRS, PAIR, 2 * tq), BF16),
                        pltpu.VMEM((IDX_HEADS // 2, IDX_DIM, 2 * tq), BF16),
                        pltpu.VMEM((N_PAIRS, 1, 2 * tq), F32),
                        pltpu.VMEM((N_PAIRS, 1, 2 * tq), F32),
                        pltpu.VMEM((N_PAIRS, PAIR, tq), F32)]
                       + [pltpu.VMEM((2, tk, tq), F32)] * (SCORE_LOOKAHEAD + 1),
        compiler_params=_params(("parallel", "arbitrary")),
    )(qt, qit, wit, k, vt, ki)


def kernel(x_prompt, x_sample, cache_a_k, cache_a_v, cache_b_k, cache_b_v, cache_b_kidx,
           w_a_qkv, w_a_o, a_rel_bias, w_b_in, w_b_o, w_ff1, w_ff2, ln1_g, ln1_b, ln2_g, ln2_b):
    bsz, seq, _ = x_prompt.shape
    dbsz, dseq, _ = x_sample.shape
    depth = w_ff1.shape[0]
    past = cache_b_k.shape[2]
    alpha = float((2.0 * depth) ** 0.25)
    n_p = bsz * seq
    n_s = dbsz * dseq
    q_scale = float(HEAD_DIM ** -0.5) * LOG2E

    x = jnp.concatenate([x_prompt.reshape(n_p, D_MODEL), x_sample.reshape(n_s, D_MODEL)], axis=0)

    def split(a, width):
        return a[:n_p].reshape(bsz, seq, width), a[n_p:].reshape(dbsz, dseq, width)

    def heads(a):
        return a.reshape(a.shape[:-1] + (N_HEADS, HEAD_DIM))

    ak_p, av_p, ak_s, av_s = [], [], [], []
    bk_p, bv_p, bi_p, bk_s, bv_s, bi_s = [], [], [], [], [], []
    for i in range(depth):
        j = i // 2
        if i % 2 == 0:
            groups = [(0, D_MODEL, [(0, q_scale, "rows_tail"), (1, q_scale, "cols")]),
                      (D_MODEL, D_MODEL, [(2, 1.0, "rows_head"), (3, 1.0, "rows_tail"), (4, 1.0, "rows")]),
                      (2 * D_MODEL, D_MODEL, [(5, 1.0, "rows_head"), (6, 1.0, "rows_tail"), (7, 1.0, "rows_tail"),
                                              (8, 1.0, "pair_cols")])]
            defs = [(D_MODEL, BF16, "rows_tail"), (D_MODEL, BF16, "cols"),
                    (D_MODEL, F32, "rows_head"), (D_MODEL, F32, "rows_tail"), (D_MODEL, BF16, "rows"),
                    (D_MODEL, F32, "rows_head"), (D_MODEL, F32, "rows_tail"), (D_MODEL, BF16, "rows_tail"),
                    (N_PAIRS * VT_ROWS, BF16, "pair_cols")]
            q_s, qt, k32_p, k32_s, k16, v32_p, v32_s, v_s, vt = _proj(x, w_a_qkv[j], groups, defs, n_p)
            o_p = _band_prompt(qt, k16, vt, a_rel_bias[j], n_batch=bsz, seq=seq)
            q_s = q_s.reshape(dbsz, dseq, D_MODEL)
            k_s = k16[n_p:].reshape(dbsz, dseq, D_MODEL)
            v_s = v_s.reshape(dbsz, dseq, D_MODEL)
            a_len = cache_a_k.shape[2]
            ck = cache_a_k[j].reshape(dbsz, a_len, D_MODEL).astype(BF16)
            cv = cache_a_v[j].reshape(dbsz, a_len, D_MODEL).astype(BF16)
            o_s = _band_sample(q_s, k_s, v_s, ck, cv, a_rel_bias[j], past).reshape(n_s, D_MODEL)
            n_keep = min(N_PREV_CHUNKS * CHUNK, seq)
            ak_p.append(heads(k32_p.reshape(bsz, seq, D_MODEL)[:, seq - n_keep:]))
            av_p.append(heads(v32_p.reshape(bsz, seq, D_MODEL)[:, seq - n_keep:]))
            ak_s.append(heads(k32_s.reshape(dbsz, dseq, D_MODEL)))
            av_s.append(heads(v32_s.reshape(dbsz, dseq, D_MODEL)))
            w_o = w_a_o[j]
        else:
            w_in = w_b_in[j]
            w_cat = jnp.concatenate(
                [w_in[:, :3 * D_MODEL], jnp.pad(w_in[:, 3 * D_MODEL:], ((0, 0), (0, IDX_PAD - IDX_WIDTH)))],
                axis=1)
            n_qi = IDX_HEADS * IDX_DIM
            groups = [(0, D_MODEL, [(0, q_scale, "cols")]),
                      (D_MODEL, D_MODEL, [(1, 1.0, "rows_head"), (2, 1.0, "rows_tail"), (3, 1.0, "rows")]),
                      (2 * D_MODEL, D_MODEL, [(4, 1.0, "rows_head"), (5, 1.0, "rows_tail"), (6, 1.0, "pair_cols")]),
                      (3 * D_MODEL, n_qi, [(7, 1.0, "cols")]),
                      (3 * D_MODEL + n_qi, IDX_PAD - n_qi, [(8, 1.0, "rows"), (9, 1.0, "cols")])]
            defs = [(D_MODEL, BF16, "cols"),
                    (D_MODEL, F32, "rows_head"), (D_MODEL, F32, "rows_tail"), (D_MODEL, BF16, "rows"),
                    (D_MODEL, F32, "rows_head"), (D_MODEL, F32, "rows_tail"), (N_PAIRS * VT_ROWS, BF16, "pair_cols"),
                    (n_qi, BF16, "cols"), (IDX_PAD - n_qi, F32, "rows"), (IDX_PAD - n_qi, F32, "cols")]
            qt, k32_p, k32_s, k16, v32_p, v32_s, vt, qit, kiw, kiwt = _proj(x, w_cat, groups, defs, n_p)
            wit = kiwt[IDX_DIM:IDX_DIM + IDX_HEADS]
            ki32 = kiw[:, :IDX_DIM]
            ki16 = ki32.astype(BF16)

            tq_p = _row_tile(seq, DSA_PROMPT_QUERIES)
            o_p = _dsa(qt, qit, wit, k16, vt, ki16, n_batch=bsz, n_q=seq, l_pad=seq, tq=tq_p, n_q_valid=tq_p,
                       q_off=0, n_keys=seq, n_sel=min(TOPK_MAX, seq // 4), resident=True)
            tq = 2 * CHUNK

            n_keys = past + dseq
            l_pad = -(-n_keys // TOPK_MAX) * TOPK_MAX
            tq_s = -(-dseq // tq) * tq

            def pad_q(a):
                w = a.shape[0]
                a = jnp.pad(a[:, n_p:].reshape(w, dbsz, dseq), ((0, 0), (0, 0), (0, tq_s - dseq)))
                return a.reshape(w, dbsz * tq_s)

            def keys_rows(cache, new):
                w = new.shape[-1]
                a = jnp.concatenate([cache.astype(BF16), new.reshape(dbsz, dseq, w)], axis=1)
                return jnp.pad(a, ((0, 0), (0, l_pad - n_keys), (0, 0))).reshape(dbsz * l_pad, w)

            k_all = keys_rows(cache_b_k[j].reshape(dbsz, past, D_MODEL), k16[n_p:])
            ki_all = keys_rows(cache_b_kidx[j], ki16[n_p:])
            cvt = jnp.transpose(cache_b_v[j].reshape(dbsz, past, N_PAIRS, PAIR).astype(BF16), (2, 3, 0, 1))
            cvt = jnp.concatenate([cvt, jnp.ones((N_PAIRS, ONES_ROWS, dbsz, past), BF16)], axis=1)
            vt_all = jnp.concatenate([cvt.reshape(N_PAIRS * VT_ROWS, dbsz, past),
                                      vt[:, n_p:].reshape(N_PAIRS * VT_ROWS, dbsz, dseq)], axis=2)
            vt_all = jnp.pad(vt_all, ((0, 0), (0, 0), (0, l_pad - n_keys))).reshape(N_PAIRS * VT_ROWS, dbsz * l_pad)
            o_s = _dsa(pad_q(qt), pad_q(qit), pad_q(wit), k_all, vt_all, ki_all, n_batch=dbsz, n_q=tq_s,
                       l_pad=l_pad, tq=tq, n_q_valid=dseq, q_off=past, n_keys=n_keys,
                       n_sel=min(TOPK_MAX, n_keys // 4), resident=False)
            o_s = o_s.reshape(dbsz, tq_s, D_MODEL)[:, :dseq].reshape(n_s, D_MODEL)

            ki_p, ki_s = split(ki32, IDX_DIM)
            bk_p.append(heads(k32_p.reshape(bsz, seq, D_MODEL))); bv_p.append(heads(v32_p.reshape(bsz, seq, D_MODEL)))
            bk_s.append(heads(k32_s.reshape(dbsz, dseq, D_MODEL))); bv_s.append(heads(v32_s.reshape(dbsz, dseq, D_MODEL)))
            bi_p.append(ki_p); bi_s.append(ki_s)
            w_o = w_b_o[j]

        x = _oproj_ln(o_p, o_s, w_o, x, ln1_g[i], ln1_b[i], alpha)
        x = _ffn_ln(x, w_ff1[i], w_ff2[i], ln2_g[i], ln2_b[i], alpha)

    y_p, y_s = split(x, D_MODEL)
    return (y_p, y_s,
            jnp.stack(ak_p), jnp.stack(av_p), jnp.stack(bk_p), jnp.stack(bv_p), jnp.stack(bi_p),
            jnp.stack(ak_s), jnp.stack(av_s), jnp.stack(bk_s), jnp.stack(bv_s), jnp.stack(bi_s))
```
